```python
import jax, jax.numpy as jnp
from jax import lax
import numpy as np

D_MODEL = 4096
BATCH = 2
SEQ = 4096
DEPTH = 4

CTX_LEN = 256
GRID_W = 64
N_MIXERS = 2
N_LRU_LAYERS = (DEPTH + N_MIXERS - 1) // N_MIXERS
N_ATTN_LAYERS = DEPTH // N_MIXERS
N_ADA = 6
D_RNN = D_MODEL
CONV_W = 4
CONV_PAD = (1, 2)
LRU_BLOCKS = 16
LRU_BLOCK_W = D_RNN // LRU_BLOCKS
LRU_C = 8.0
HEAD_DIM = 128
N_HEADS = D_MODEL // HEAD_DIM
N_KV_HEADS = N_HEADS // 4
Q_BLOCK = 128
ROPE_THETA = 10000.0
N_GROUPS = 4
EXPERTS_PER_GROUP = 4
N_EXPERTS = N_GROUPS * EXPERTS_PER_GROUP
TOP_K = 2
D_FF_EXPERT = 384
EPS = 1e-6

kernel_name = 'hybrid_rglru_gqa_hmoe_dit_prefix'


def rmsnorm(x, g):
    xf = x.astype(jnp.float32)
    y = xf * lax.rsqrt(jnp.mean(xf * xf, axis=-1, keepdims=True) + EPS)
    return (y * g.astype(jnp.float32)).astype(x.dtype)


def ada_params(cond, w, b):
    m = jax.nn.silu(cond) @ w + b
    return jnp.split(m[..., None, :], w.shape[-1] // D_MODEL, axis=-1)


def modulate(x, g, shift, scale):
    return rmsnorm(x, g) * (1.0 + scale) + shift


def dwconv(x, w, b):
    out = lax.conv_general_dilated(x, w[:, None, :], window_strides=(1,), padding=[CONV_PAD],
                                   dimension_numbers=('NWC', 'WIO', 'NWC'),
                                   feature_group_count=x.shape[-1])
    return out + b


def _lin_combine(e1, e2):
    a1, b1 = e1
    a2, b2 = e2
    return a1 * a2, a2 * b1 + b2


def linear_scan(a, u, h0):
    A, H = lax.associative_scan(_lin_combine, (a, u), axis=1)
    return A * h0[:, None, :] + H


def lru_coeffs(x, w_a, b_a, w_i, b_i, lam):
    B_, T, _ = x.shape
    xb = x.reshape(B_, T, LRU_BLOCKS, LRU_BLOCK_W)
    r = jax.nn.sigmoid(jnp.einsum('btki,kij->btkj', xb, w_a).reshape(B_, T, D_RNN) + b_a)
    i = jax.nn.sigmoid(jnp.einsum('btki,kij->btkj', xb, w_i).reshape(B_, T, D_RNN) + b_i)
    log_a = -LRU_C * r.astype(jnp.float32) * jax.nn.softplus(-lam.astype(jnp.float32))
    a = jnp.exp(log_a)
    u = jnp.sqrt(-jnp.expm1(2.0 * log_a)) * (i * x).astype(jnp.float32)
    return a, u


def rglru_mixer(h_lat, h_ctx, w_in, conv_w, conv_b, w_a, b_a, w_i, b_i, lam, w_out, need_ctx_out):
    B_ = h_lat.shape[0]
    y_l, xr_l = jnp.split(h_lat @ w_in, 2, axis=-1)
    x_l = dwconv(xr_l, conv_w, conv_b)
    x_c = dwconv(h_ctx @ w_in[:, D_RNN:], conv_w, conv_b)
    out_l = jnp.zeros(x_l.shape, jnp.float32)
    out_c = jnp.zeros(x_c.shape, jnp.float32)
    h0 = jnp.zeros((B_, D_RNN), jnp.float32)
    for d in range(2):
        flip = (lambda t: jnp.flip(t, axis=1)) if d == 1 else (lambda t: t)
        a_c, u_c = lru_coeffs(flip(x_c), w_a[d], b_a[d], w_i[d], b_i[d], lam[d])
        hc = linear_scan(a_c, u_c, h0)
        a_l, u_l = lru_coeffs(flip(x_l), w_a[d], b_a[d], w_i[d], b_i[d], lam[d])
        hl = linear_scan(a_l, u_l, hc[:, -1])
        out_l = out_l + flip(hl)
        out_c = out_c + flip(hc)
    lat = (out_l.astype(h_lat.dtype) * jax.nn.gelu(y_l)) @ w_out
    ctx_out = None
    if need_ctx_out:
        y_c = h_ctx @ w_in[:, :D_RNN]
        ctx_out = (out_c.astype(h_ctx.dtype) * jax.nn.gelu(y_c)) @ w_out
    return lat, ctx_out


def rope_tables(S):
    rows = S // GRID_W
    t_row = jnp.repeat(jnp.arange(rows), GRID_W).astype(jnp.float32)
    t_col = jnp.tile(jnp.arange(GRID_W), rows).astype(jnp.float32)
    n_f = HEAD_DIM // 4
    inv = ROPE_THETA ** (-jnp.arange(n_f, dtype=jnp.float32) / n_f)
    ang = jnp.concatenate([t_row[:, None] * inv, t_col[:, None] * inv], axis=-1)
    return jnp.cos(ang), jnp.sin(ang)


def apply_rope(x, cos, sin):
    xf = x.astype(jnp.float32)
    x1, x2 = jnp.split(xf, 2, axis=-1)
    c = cos[None, :, None, :]
    s = sin[None, :, None, :]
    return jnp.concatenate([x1 * c - x2 * s, x2 * c + x1 * s], axis=-1).astype(x.dtype)


def attend(q, k, v):
    B_, T, _, _ = q.shape
    qg = q.reshape(B_, T, N_KV_HEADS, N_HEADS // N_KV_HEADS, HEAD_DIM)
    s = jnp.einsum('btkgd,blkd->bkgtl', qg, k).astype(jnp.float32) * (HEAD_DIM ** -0.5)
    p = jax.nn.softmax(s, axis=-1).astype(v.dtype)
    o = jnp.einsum('bkgtl,blkd->btkgd', p, v)
    return o.reshape(B_, T, N_HEADS * HEAD_DIM)


def block_attention(q, k, v):
    B_, S, H, Dh = q.shape
    nb = S // Q_BLOCK
    qb = jnp.moveaxis(q.reshape(B_, nb, Q_BLOCK, H, Dh), 1, 0)
    out = lax.map(lambda qi: attend(qi, k, v), qb)
    return jnp.moveaxis(out, 0, 1).reshape(B_, S, H * Dh)


def gqa_mixer(h_lat, h_ctx, w_qkv, q_norm, k_norm, w_o, cos, sin, need_ctx_out):
    qd = N_HEADS * HEAD_DIM
    kvd = N_KV_HEADS * HEAD_DIM
    B_, S, _ = h_lat.shape
    C = h_ctx.shape[1]
    q_l, k_l, v_l = jnp.split(h_lat @ w_qkv, [qd, qd + kvd], axis=-1)
    q_l = apply_rope(rmsnorm(q_l.reshape(B_, S, N_HEADS, HEAD_DIM), q_norm), cos, sin)
    k_l = apply_rope(rmsnorm(k_l.reshape(B_, S, N_KV_HEADS, HEAD_DIM), k_norm), cos, sin)
    v_l = v_l.reshape(B_, S, N_KV_HEADS, HEAD_DIM)
    k_c, v_c = jnp.split(h_ctx @ w_qkv[:, qd:], 2, axis=-1)
    k_c = rmsnorm(k_c.reshape(B_, C, N_KV_HEADS, HEAD_DIM), k_norm)
    v_c = v_c.reshape(B_, C, N_KV_HEADS, HEAD_DIM)
    k_all = jnp.concatenate([k_c, k_l], axis=1)
    v_all = jnp.concatenate([v_c, v_l], axis=1)
    lat = block_attention(q_l, k_all, v_all) @ w_o
    ctx_out = None
    if need_ctx_out:
        q_c = rmsnorm((h_ctx @ w_qkv[:, :qd]).reshape(B_, C, N_HEADS, HEAD_DIM), q_norm)
        ctx_out = attend(q_c, k_c, v_c) @ w_o
    return lat, ctx_out


def hier_moe(x, w_rg, b_rg, w_re, b_re, w1, w3, w2):
    shp = x.shape
    xt = x.reshape(-1, D_MODEL)
    xf = xt.astype(jnp.float32)
    g_prob = jax.nn.softmax(xf @ w_rg.astype(jnp.float32) + b_rg.astype(jnp.float32), axis=-1)
    g_top, g_idx = lax.top_k(g_prob, 1)
    e_logits = (xf @ w_re.astype(jnp.float32) + b_re.astype(jnp.float32)).reshape(-1, N_GROUPS, EXPERTS_PER_GROUP)
    e_sel = jnp.take_along_axis(e_logits, g_idx[:, :, None], axis=1)[:, 0]
    e_top, e_idx = lax.top_k(e_sel, TOP_K)
    e_w = jax.nn.softmax(e_top, axis=-1) * g_top
    within = jnp.sum(jax.nn.one_hot(e_idx, EXPERTS_PER_GROUP, dtype=jnp.float32) * e_w[..., None], axis=1)
    gates = (jax.nn.one_hot(g_idx[:, 0], N_GROUPS, dtype=jnp.float32)[:, :, None]
             * within[:, None, :]).reshape(-1, N_EXPERTS)
    h = jax.nn.silu(jnp.einsum('nd,edf->nef', xt, w1)) * jnp.einsum('nd,edf->nef', xt, w3)
    h = h * gates.astype(x.dtype)[..., None]
    return jnp.einsum('nef,efd->nd', h, w2).reshape(shp)


def setup_inputs(seed: int = 0) -> dict:
    key = jax.random.key(seed)
    ks = jax.random.split(key, 32)
    f32 = jnp.float32
    D, R = D_MODEL, D_RNN
    QKV = (N_HEADS + 2 * N_KV_HEADS) * HEAD_DIM

    def nrm(k, shape, fan):
        return jax.random.normal(k, shape, f32) * (fan ** -0.5)

    def gain(k, shape):
        return 1.0 + 0.02 * jax.random.normal(k, shape, f32)

    def bias(k, shape, s=0.02):
        return s * jax.random.normal(k, shape, f32)

    u = jax.random.uniform(ks[12], (N_LRU_LAYERS, 2, R), f32, 0.9, 0.999)
    a0 = u ** (1.0 / LRU_C)
    lam = jnp.log(a0) - jnp.log1p(-a0)
    return {
        'x': jax.random.normal(ks[0], (BATCH, SEQ, D), f32),
        'c': jax.random.normal(ks[1], (BATCH, D), f32),
        'ctx': jax.random.normal(ks[2], (BATCH, CTX_LEN, D), f32),
        'c_ctx': jax.random.normal(ks[3], (D,), f32),
        'ada_w': nrm(ks[4], (DEPTH, D, N_ADA * D), D),
        'ada_b': bias(ks[5], (DEPTH, N_ADA * D)),
        'norm_mix': gain(ks[6], (DEPTH, D)),
        'norm_ffn': gain(ks[7], (DEPTH, D)),
        'final_norm': gain(ks[8], (D,)),
        'lru_w_in': nrm(ks[9], (N_LRU_LAYERS, D, 2 * R), D),
        'lru_conv_w': nrm(ks[10], (N_LRU_LAYERS, CONV_W, R), CONV_W),
        'lru_conv_b': bias(ks[11], (N_LRU_LAYERS, R)),
        'lru_w_a': nrm(ks[13], (N_LRU_LAYERS, 2, LRU_BLOCKS, LRU_BLOCK_W, LRU_BLOCK_W), LRU_BLOCK_W),
        'lru_b_a': bias(ks[14], (N_LRU_LAYERS, 2, R)),
        'lru_w_i': nrm(ks[15], (N_LRU_LAYERS, 2, LRU_BLOCKS, LRU_BLOCK_W, LRU_BLOCK_W), LRU_BLOCK_W),
        'lru_b_i': bias(ks[16], (N_LRU_LAYERS, 2, R)),
        'lru_lam': lam,
        'lru_w_out': nrm(ks[17], (N_LRU_LAYERS, R, D), R),
        'attn_w_qkv': nrm(ks[18], (N_ATTN_LAYERS, D, QKV), D),
        'attn_q_norm': gain(ks[19], (N_ATTN_LAYERS, HEAD_DIM)),
        'attn_k_norm': gain(ks[20], (N_ATTN_LAYERS, HEAD_DIM)),
        'attn_w_o': nrm(ks[21], (N_ATTN_LAYERS, N_HEADS * HEAD_DIM, D), N_HEADS * HEAD_DIM),
        'moe_w_rg': nrm(ks[22], (DEPTH, D, N_GROUPS), D),
        'moe_b_rg': bias(ks[23], (DEPTH, N_GROUPS), 0.01),
        'moe_w_re': nrm(ks[24], (DEPTH, D, N_EXPERTS), D),
        'moe_b_re': bias(ks[25], (DEPTH, N_EXPERTS), 0.01),
        'moe_w1': nrm(ks[26], (DEPTH, N_EXPERTS, D, D_FF_EXPERT), D),
        'moe_w3': nrm(ks[27], (DEPTH, N_EXPERTS, D, D_FF_EXPERT), D),
        'moe_w2': nrm(ks[28], (DEPTH, N_EXPERTS, D_FF_EXPERT, D), D_FF_EXPERT),
    }


def reference(x, c, ctx, c_ctx, ada_w, ada_b, norm_mix, norm_ffn, final_norm,
              lru_w_in, lru_conv_w, lru_conv_b, lru_w_a, lru_b_a, lru_w_i, lru_b_i, lru_lam, lru_w_out,
              attn_w_qkv, attn_q_norm, attn_k_norm, attn_w_o,
              moe_w_rg, moe_b_rg, moe_w_re, moe_b_re, moe_w1, moe_w3, moe_w2):
    cos, sin = rope_tables(x.shape[1])
    xl, xc = x, ctx
    for l in range(DEPTH):
        last = l == DEPTH - 1
        sh_m, sc_m, g_m, sh_f, sc_f, g_f = ada_params(c, ada_w[l], ada_b[l])
        n_ctx_mod = 3 if last else N_ADA
        cmod = ada_params(c_ctx, ada_w[l][:, :n_ctx_mod * D_MODEL], ada_b[l][:n_ctx_mod * D_MODEL])
        hl = modulate(xl, norm_mix[l], sh_m, sc_m)
        hc = modulate(xc, norm_mix[l], cmod[0], cmod[1])
        j = l // N_MIXERS
        if l % N_MIXERS == 0:
            ml, mc = rglru_mixer(hl, hc, lru_w_in[j], lru_conv_w[j], lru_conv_b[j], lru_w_a[j], lru_b_a[j],
                                 lru_w_i[j], lru_b_i[j], lru_lam[j], lru_w_out[j], not last)
        else:
            ml, mc = gqa_mixer(hl, hc, attn_w_qkv[j], attn_q_norm[j], attn_k_norm[j], attn_w_o[j],
                               cos, sin, not last)
        xl = xl + g_m * ml
        hl = modulate(xl, norm_ffn[l], sh_f, sc_f)
        xl = xl + g_f * hier_moe(hl, moe_w_rg[l], moe_b_rg[l], moe_w_re[l], moe_b_re[l],
                                 moe_w1[l], moe_w3[l], moe_w2[l])
        if not last:
            xc = xc + cmod[2] * mc
            hc = modulate(xc, norm_ffn[l], cmod[3], cmod[4])
            xc = xc + cmod[5] * hier_moe(hc, moe_w_rg[l], moe_b_rg[l], moe_w_re[l], moe_b_re[l],
                                         moe_w1[l], moe_w3[l], moe_w2[l])
    return rmsnorm(xl, final_norm)
```

```python
import functools
import math

import jax
import jax.numpy as jnp
from jax import lax
from jax.experimental import pallas as pl
from jax.experimental.pallas import tpu as pltpu

EPS = 1e-6
LRU_C = 8.0
ROPE_THETA = 10000.0
ROPE_GRID_W = 64
LANES = 128
NCLS_PAD = 8
V7X_VMEM_CAP = 56 * 1024 * 1024
BF16 = jnp.bfloat16
F32 = jnp.float32


def _vmem_limit(nbytes):
    return int(min(V7X_VMEM_CAP, max(16 * 1024 * 1024, nbytes * 5 // 4 + (4 << 20))))


def _params(sem, nbytes):
    return pltpu.CompilerParams(dimension_semantics=sem, vmem_limit_bytes=_vmem_limit(nbytes))


def _sigmoid(x):
    return 1.0 / (1.0 + jnp.exp(-x))


def _gelu_tanh(x):
    return 0.5 * x * (1.0 + jnp.tanh(math.sqrt(2.0 / math.pi) * (x + 0.044715 * (x * x * x))))


def _ada_kernel(c_ref, w_ref, b_ref, o_ref):
    c = c_ref[...]
    s = (c * _sigmoid(c)).astype(BF16)
    w = w_ref[...].astype(BF16)
    o_ref[...] = jnp.dot(s, w, preferred_element_type=F32) + b_ref[...]


def ada_all(cond, ada_w, ada_b):
    depth, d, n = ada_w.shape
    tn = 512
    return pl.pallas_call(
        _ada_kernel,
        grid=(depth, n // tn),
        in_specs=[
            pl.BlockSpec((NCLS_PAD, d), lambda l, j: (0, 0)),
            pl.BlockSpec((None, d, tn), lambda l, j: (l, 0, j)),
            pl.BlockSpec((None, 1, tn), lambda l, j: (l, 0, j)),
        ],
        out_specs=pl.BlockSpec((None, NCLS_PAD, tn), lambda l, j: (l, 0, j)),
        out_shape=jax.ShapeDtypeStruct((depth, NCLS_PAD, n), F32),
        compiler_params=_params(("arbitrary", "arbitrary"), 2 * d * tn * 4 + d * tn * 2),
        name="ada_all",
    )(cond, ada_w, ada_b.reshape(depth, 1, n))


def _modulate(x, gs, sh):
    ms = jnp.mean(x * x, axis=-1, keepdims=True)
    return x * lax.rsqrt(ms + EPS) * gs + sh


def _mod_kernel(x_ref, gs_ref, sh_ref, o_ref):
    o_ref[...] = _modulate(x_ref[...], gs_ref[...], sh_ref[...]).astype(o_ref.dtype)


def _mod_router_kernel(n_groups, per_group, x_ref, gs_ref, sh_ref, wr_ref, br_ref, h_ref, r_ref):
    h = _modulate(x_ref[...], gs_ref[...], sh_ref[...])
    h_ref[...] = h
    logits = jnp.dot(h, wr_ref[...], preferred_element_type=F32,
                     precision=lax.Precision.HIGHEST) + br_ref[...]
    lane = lax.broadcasted_iota(jnp.int32, logits.shape, 1)
    neg = jnp.float32(-jnp.inf)
    big = jnp.int32(1 << 20)
    gl = jnp.where(lane < n_groups, logits, neg)
    gmax = jnp.max(gl, axis=-1, keepdims=True)
    gsum = jnp.sum(jnp.exp(gl - gmax), axis=-1, keepdims=True)
    g_top = 1.0 / gsum
    g_idx = jnp.min(jnp.where(gl == gmax, lane, big), axis=-1, keepdims=True)
    lo = n_groups + per_group * g_idx
    el = jnp.where((lane >= lo) & (lane < lo + per_group), logits, neg)
    m1 = jnp.max(el, axis=-1, keepdims=True)
    i1 = jnp.min(jnp.where(el == m1, lane, big), axis=-1, keepdims=True)
    el2 = jnp.where(lane == i1, neg, el)
    m2 = jnp.max(el2, axis=-1, keepdims=True)
    i2 = jnp.min(jnp.where(el2 == m2, lane, big), axis=-1, keepdims=True)
    t = jnp.exp(m2 - m1)
    w1 = g_top / (1.0 + t)
    w2 = g_top * t / (1.0 + t)
    e1 = (i1 - n_groups).astype(F32)
    e2 = (i2 - n_groups).astype(F32)
    r_ref[...] = jnp.where(lane == 0, e1, jnp.where(lane == 1, e2,
                           jnp.where(lane == 2, w1, jnp.where(lane == 3, w2, 0.0))))


def _cls_of_tile(tm, s_len, n_lat, n_batch):
    def f(i):
        r0 = i * tm
        return jnp.where(r0 < n_lat, r0 // s_len, n_batch)
    return f


def modulate(x, gs, sh, *, s_len, n_batch, tm=256):
    r, d = x.shape
    cls = _cls_of_tile(tm, s_len, s_len * n_batch, n_batch)
    return pl.pallas_call(
        _mod_kernel,
        grid=(r // tm,),
        in_specs=[
            pl.BlockSpec((tm, d), lambda i: (i, 0)),
            pl.BlockSpec((None, 1, d), lambda i: (cls(i), 0, 0)),
            pl.BlockSpec((None, 1, d), lambda i: (cls(i), 0, 0)),
        ],
        out_specs=pl.BlockSpec((tm, d), lambda i: (i, 0)),
        out_shape=jax.ShapeDtypeStruct((r, d), BF16),
        compiler_params=_params(("arbitrary",), 2 * tm * d * 6),
        name="modulate",
    )(x, gs, sh)


def modulate_router(x, gs, sh, wr, br, *, n_groups, per_group, s_len, n_batch, tm=256):
    r, d = x.shape
    cls = _cls_of_tile(tm, s_len, s_len * n_batch, n_batch)
    return pl.pallas_call(
        functools.partial(_mod_router_kernel, n_groups, per_group),
        grid=(r // tm,),
        in_specs=[
            pl.BlockSpec((tm, d), lambda i: (i, 0)),
            pl.BlockSpec((None, 1, d), lambda i: (cls(i), 0, 0)),
            pl.BlockSpec((None, 1, d), lambda i: (cls(i), 0, 0)),
            pl.BlockSpec((d, LANES), lambda i: (0, 0)),
            pl.BlockSpec((1, LANES), lambda i: (0, 0)),
        ],
        out_specs=[
            pl.BlockSpec((tm, d), lambda i: (i, 0)),
            pl.BlockSpec((tm, LANES), lambda i: (i, 0)),
        ],
        out_shape=[jax.ShapeDtypeStruct((r, d), F32), jax.ShapeDtypeStruct((r, LANES), F32)],
        compiler_params=_params(("arbitrary",), 2 * tm * d * 8 + 2 * d * LANES * 4 + 4 * tm * d * 4),
        name="modulate_router",
    )(x, gs, sh, wr, br)


def _mm_plain_kernel(act, a_ref, w_ref, o_ref):
    acc = jnp.dot(a_ref[...], w_ref[...], preferred_element_type=F32)
    if act == "gelu":
        acc = _gelu_tanh(acc)
    o_ref[...] = acc.astype(o_ref.dtype)


def _mm_resid_kernel(a_ref, w_ref, x_ref, g_ref, o_ref):
    acc = jnp.dot(a_ref[...], w_ref[...], preferred_element_type=F32)
    o_ref[...] = x_ref[...] + g_ref[...] * acc


def _mm_qkv_kernel(n_norm_tiles, hd, a_ref, w_ref, gain_ref, cos_ref, sin_ref, o_ref):
    acc = jnp.dot(a_ref[...], w_ref[...], preferred_element_type=F32)
    j = pl.program_id(1)

    @pl.when(j < n_norm_tiles)
    def _():
        cos = cos_ref[...]
        sin = sin_ref[...]
        for h in range(acc.shape[1] // hd):
            xh = acc[:, h * hd:(h + 1) * hd]
            ms = jnp.mean(xh * xh, axis=-1, keepdims=True)
            y = xh * lax.rsqrt(ms + EPS) * gain_ref[:, h * hd:(h + 1) * hd]
            y = y * cos + pltpu.roll(y, hd // 2, 1) * sin
            o_ref[:, h * hd:(h + 1) * hd] = y.astype(o_ref.dtype)

    @pl.when(j >= n_norm_tiles)
    def _():
        o_ref[...] = acc.astype(o_ref.dtype)


def _mm_tiles(r, n, tm, tn):
    assert r % tm == 0 and n % tn == 0, (r, n, tm, tn)
    return (r // tm, n // tn)


def matmul(a, w, *, out_dtype, act=None, tm=512, tn=1024):
    r, k = a.shape
    n = w.shape[1]
    tn = min(tn, n)
    ob = jnp.dtype(out_dtype).itemsize
    return pl.pallas_call(
        functools.partial(_mm_plain_kernel, act),
        grid=_mm_tiles(r, n, tm, tn),
        in_specs=[pl.BlockSpec((tm, k), lambda i, j: (i, 0)),
                  pl.BlockSpec((k, tn), lambda i, j: (0, j))],
        out_specs=pl.BlockSpec((tm, tn), lambda i, j: (i, j)),
        out_shape=jax.ShapeDtypeStruct((r, n), out_dtype),
        compiler_params=_params(("arbitrary", "arbitrary"),
                                2 * (tm * k * 2 + k * tn * 2 + tm * tn * ob) + 2 * tm * tn * 4),
        name="matmul_" + (act or "plain"),
    )(a, w)


def matmul_resid(a, w, xres, gate, *, s_len, n_batch, tm=512, tn=1024):
    r, k = a.shape
    n = w.shape[1]
    tn = min(tn, n)
    cls = _cls_of_tile(tm, s_len, s_len * n_batch, n_batch)
    return pl.pallas_call(
        _mm_resid_kernel,
        grid=_mm_tiles(r, n, tm, tn),
        in_specs=[pl.BlockSpec((tm, k), lambda i, j: (i, 0)),
                  pl.BlockSpec((k, tn), lambda i, j: (0, j)),
                  pl.BlockSpec((tm, tn), lambda i, j: (i, j)),
                  pl.BlockSpec((None, 1, tn), lambda i, j: (cls(i), 0, j))],
        out_specs=pl.BlockSpec((tm, tn), lambda i, j: (i, j)),
        out_shape=jax.ShapeDtypeStruct((r, n), F32),
        compiler_params=_params(("arbitrary", "arbitrary"),
                                2 * (tm * k * 2 + k * tn * 2 + 2 * tm * tn * 4) + 2 * tm * tn * 4),
        name="matmul_resid",
    )(a, w, xres, gate)


def matmul_qkv(a, w, gain, cos_tbl, sin_tbl, *, n_norm_cols, hd, s_len, n_batch, tm=512, tn=1024):
    r, k = a.shape
    n = w.shape[1]
    while n_norm_cols % tn or n % tn:
        tn //= 2
    assert tn % hd == 0 and s_len % tm == 0
    n_lat = s_len * n_batch
    pos_blocks = s_len // tm

    def pos(i):
        return jnp.where(i * tm < n_lat, i % pos_blocks, pos_blocks)

    return pl.pallas_call(
        functools.partial(_mm_qkv_kernel, n_norm_cols // tn, hd),
        grid=_mm_tiles(r, n, tm, tn),
        in_specs=[pl.BlockSpec((tm, k), lambda i, j: (i, 0)),
                  pl.BlockSpec((k, tn), lambda i, j: (0, j)),
                  pl.BlockSpec((1, tn), lambda i, j: (0, j)),
                  pl.BlockSpec((tm, hd), lambda i, j: (pos(i), 0)),
                  pl.BlockSpec((tm, hd), lambda i, j: (pos(i), 0))],
        out_specs=pl.BlockSpec((tm, tn), lambda i, j: (i, j)),
        out_shape=jax.ShapeDtypeStruct((r, n), BF16),
        compiler_params=_params(("arbitrary", "arbitrary"),
                                2 * (tm * k * 2 + k * tn * 2 + tm * tn * 2 + 2 * tm * hd * 4)
                                + 3 * tm * tn * 4),
        name="matmul_qkv",
    )(a, w, gain, cos_tbl, sin_tbl)


def _scan_chunk(a, u, reverse):
    n = a.shape[0]
    rows = lax.broadcasted_iota(jnp.int32, a.shape, 0)
    sh = 1
    while sh < n:
        if reverse:
            a_s = pltpu.roll(a, n - sh, 0)
            u_s = pltpu.roll(u, n - sh, 0)
            m = rows < n - sh
        else:
            a_s = pltpu.roll(a, sh, 0)
            u_s = pltpu.roll(u, sh, 0)
            m = rows >= sh
        u = jnp.where(m, a * u_s + u, u)
        a = jnp.where(m, a * a_s, a)
        sh *= 2
    return a, u


def _lru_kernel(tc, xl_ref, xc_ref, yl_ref, yc_ref, cw_ref, cb_ref, wa_ref, wi_ref, ba_ref, bi_ref,
                lam_ref, ol_ref, oc_ref, cvl_ref, cvc_ref, accl_ref, accc_ref):
    w = xl_ref.shape[1]
    cw = cw_ref[...]
    cb = cb_ref[...]

    def conv_seq(x_ref, cv_ref):
        t_len = x_ref.shape[0]
        nchunk = t_len // tc

        def body(ci, _):
            t0 = pl.multiple_of(ci * tc, tc)
            main = x_ref[pl.ds(t0, tc), :]
            p0 = pl.multiple_of(jnp.maximum(t0 - 8, 0), 8)
            n0 = pl.multiple_of(jnp.minimum(t0 + tc, t_len - 8), 8)
            prev8 = jnp.where(ci > 0, x_ref[pl.ds(p0, 8), :], 0.0)
            next8 = jnp.where(ci < nchunk - 1, x_ref[pl.ds(n0, 8), :], 0.0)
            ext = jnp.concatenate([prev8, main, next8], axis=0)
            ne = tc + 16
            xm1 = pltpu.roll(ext, 1, 0)[8:8 + tc]
            xp1 = pltpu.roll(ext, ne - 1, 0)[8:8 + tc]
            xp2 = pltpu.roll(ext, ne - 2, 0)[8:8 + tc]
            cv_ref[pl.ds(t0, tc), :] = (cw[0:1] * xm1 + cw[1:2] * main + cw[2:3] * xp1
                                        + cw[3:4] * xp2 + cb)
            return 0

        lax.fori_loop(0, nchunk, body, 0)

    conv_seq(xl_ref, cvl_ref)
    conv_seq(xc_ref, cvc_ref)

    for d in range(2):
        reverse = d == 1
        wa = wa_ref[d]
        wi = wi_ref[d]
        ba = ba_ref[d]
        bi = bi_ref[d]
        lam = lam_ref[d]
        nsp = -LRU_C * (jnp.maximum(-lam, 0.0) + jnp.log(1.0 + jnp.exp(-jnp.abs(lam))))

        def run_seq(cv_ref, acc_ref, carry, d=d, reverse=reverse, wa=wa, wi=wi, ba=ba, bi=bi, nsp=nsp):
            nchunk = cv_ref.shape[0] // tc

            def body(ci, carry):
                cidx = (nchunk - 1 - ci) if reverse else ci
                t0 = pl.multiple_of(cidx * tc, tc)
                x = cv_ref[pl.ds(t0, tc), :]
                xb = x.astype(BF16)
                r = _sigmoid(jnp.dot(xb, wa, preferred_element_type=F32) + ba)
                ig = _sigmoid(jnp.dot(xb, wi, preferred_element_type=F32) + bi)
                a = jnp.exp(r * nsp)
                u = jnp.sqrt(1.0 - a * a) * (ig * x)
                a_cum, h = _scan_chunk(a, u, reverse)
                h = h + a_cum * carry
                if d == 0:
                    acc_ref[pl.ds(t0, tc), :] = h
                else:
                    acc_ref[pl.ds(t0, tc), :] = acc_ref[pl.ds(t0, tc), :] + h
                return h[0:1, :] if reverse else h[tc - 1:tc, :]

            return lax.fori_loop(0, nchunk, body, carry)

        carry = jnp.zeros((1, w), F32)
        carry = run_seq(cvc_ref, accc_ref, carry)
        run_seq(cvl_ref, accl_ref, carry)

    def finish(acc_ref, y_ref, o_ref):
        nchunk = acc_ref.shape[0] // tc

        def body(ci, _):
            t0 = pl.multiple_of(ci * tc, tc)
            o_ref[pl.ds(t0, tc), :] = (acc_ref[pl.ds(t0, tc), :]
                                       * y_ref[pl.ds(t0, tc), :].astype(F32)).astype(o_ref.dtype)
            return 0

        lax.fori_loop(0, nchunk, body, 0)

    finish(accl_ref, yl_ref, ol_ref)
    finish(accc_ref, yc_ref, oc_ref)


def lru_mix(xr, yg, conv_w, conv_b, w_a, w_i, b_a, b_i, lam, *, s_len, c_len, n_batch, tc=256):
    r, rn = xr.shape
    nblk, bw = w_a.shape[1], w_a.shape[2]
    assert s_len % tc == 0 and c_len % tc == 0 and (n_batch * s_len) % c_len == 0
    ctx0 = n_batch * s_len // c_len
    lat = lambda b, k: (b, k)
    ctx = lambda b, k: (ctx0 + b, k)
    vec = lambda b, k: (0, 0, k)
    out_l, out_c = pl.pallas_call(
        functools.partial(_lru_kernel, tc),
        grid=(n_batch, nblk),
        in_specs=[
            pl.BlockSpec((s_len, bw), lat), pl.BlockSpec((c_len, bw), ctx),
            pl.BlockSpec((s_len, bw), lat), pl.BlockSpec((c_len, bw), ctx),
            pl.BlockSpec((conv_w.shape[0], bw), lambda b, k: (0, k)),
            pl.BlockSpec((1, bw), lambda b, k: (0, k)),
            pl.BlockSpec((2, None, bw, bw), lambda b, k: (0, k, 0, 0)),
            pl.BlockSpec((2, None, bw, bw), lambda b, k: (0, k, 0, 0)),
            pl.BlockSpec((2, 1, bw), vec), pl.BlockSpec((2, 1, bw), vec), pl.BlockSpec((2, 1, bw), vec),
        ],
        out_specs=[pl.BlockSpec((s_len, bw), lambda b, k: (b, k)),
                   pl.BlockSpec((c_len, bw), lambda b, k: (b, k))],
        out_shape=[jax.ShapeDtypeStruct((n_batch * s_len, rn), BF16),
                   jax.ShapeDtypeStruct((n_batch * c_len, rn), BF16)],
        scratch_shapes=[pltpu.VMEM((s_len, bw), F32), pltpu.VMEM((c_len, bw), F32),
                        pltpu.VMEM((s_len, bw), F32), pltpu.VMEM((c_len, bw), F32)],
        compiler_params=_params(("arbitrary", "arbitrary"),
                                (s_len + c_len) * bw * (2 * 4 + 2 * 2 + 2 * 2 + 8) + 8 * bw * bw
                                + 24 * tc * bw * 4),
        name="lru_mix",
    )(xr, xr, yg, yg, conv_w, conv_b.reshape(1, rn), w_a, w_i,
      b_a.reshape(2, 1, rn), b_i.reshape(2, 1, rn), lam.reshape(2, 1, rn))
    return jnp.concatenate([out_l, out_c], axis=0)


def _softmax_pv(q4, kv_pairs, s_ref, kc):
    m_rows = q4.shape[0]
    dn = (((1,), (1,)), ((), ()))
    mx = jnp.full((m_rows, LANES), -jnp.inf, F32)
    off = 0
    for k_ref, _ in kv_pairs:
        for c0 in range(0, k_ref.shape[0], kc):
            w = min(kc, k_ref.shape[0] - c0)
            s = lax.dot_general(q4, k_ref[c0:c0 + w, :], dn, preferred_element_type=F32)
            s_ref[:, off:off + w] = s
            for t in range(w // LANES):
                mx = jnp.maximum(mx, s[:, t * LANES:(t + 1) * LANES])
            off += w
    mrow = jnp.broadcast_to(jnp.max(mx, axis=-1, keepdims=True), (m_rows, LANES))
    lsum = jnp.zeros((m_rows, LANES), F32)
    o = jnp.zeros((m_rows, q4.shape[1]), F32)
    off = 0
    for k_ref, v_ref in kv_pairs:
        for c0 in range(0, k_ref.shape[0], kc):
            w = min(kc, k_ref.shape[0] - c0)
            ps = []
            for t in range(w // LANES):
                p = jnp.exp(s_ref[:, off + t * LANES:off + (t + 1) * LANES] - mrow)
                lsum = lsum + p
                ps.append(p.astype(BF16))
            p_all = jnp.concatenate(ps, axis=1) if len(ps) > 1 else ps[0]
            o = o + jnp.dot(p_all, v_ref[c0:c0 + w, :], preferred_element_type=F32)
            off += w
    l = jnp.sum(lsum, axis=-1, keepdims=True)
    return o / l


def _attn_kernel(n_lat_q, groups, hd, kc, q_ref, kl_ref, vl_ref, kcx_ref, vcx_ref, o_ref, s_ref):
    i = pl.program_id(2)
    tq = q_ref.shape[0]
    q = q_ref[...]
    q4 = jnp.concatenate([q[:, g * hd:(g + 1) * hd] for g in range(groups)], axis=0)

    def emit(o):
        for g in range(groups):
            o_ref[:, g * hd:(g + 1) * hd] = o[g * tq:(g + 1) * tq].astype(o_ref.dtype)

    @pl.when(i < n_lat_q)
    def _():
        emit(_softmax_pv(q4, [(kcx_ref, vcx_ref), (kl_ref, vl_ref)], s_ref, kc))

    @pl.when(i >= n_lat_q)
    def _():
        emit(_softmax_pv(q4, [(kcx_ref, vcx_ref)], s_ref, kc))


def attention(qkv, *, n_heads, n_kv, hd, s_len, c_len, n_batch, d_out, tq=128, kc=512):
    r = qkv.shape[0]
    groups = n_heads // n_kv
    gw = groups * hd
    assert s_len % tq == 0 and c_len % tq == 0 and (n_batch * s_len) % c_len == 0
    nq_lat, nq_ctx = s_len // tq, c_len // tq
    ctx_q0 = n_batch * s_len // tq
    ctx_k0 = n_batch * s_len // c_len

    def qmap(b, kv, i):
        return (jnp.where(i < nq_lat, b * nq_lat + i, ctx_q0 + b * nq_ctx + (i - nq_lat)), kv)

    m_rows = groups * tq
    return pl.pallas_call(
        functools.partial(_attn_kernel, nq_lat, groups, hd, kc),
        grid=(n_batch, n_kv, nq_lat + nq_ctx),
        in_specs=[
            pl.BlockSpec((tq, gw), qmap),
            pl.BlockSpec((s_len, hd), lambda b, kv, i: (b, n_heads + kv)),
            pl.BlockSpec((s_len, hd), lambda b, kv, i: (b, n_heads + n_kv + kv)),
            pl.BlockSpec((c_len, hd), lambda b, kv, i: (ctx_k0 + b, n_heads + kv)),
            pl.BlockSpec((c_len, hd), lambda b, kv, i: (ctx_k0 + b, n_heads + n_kv + kv)),
        ],
        out_specs=pl.BlockSpec((tq, gw), qmap),
        out_shape=jax.ShapeDtypeStruct((r, d_out), BF16),
        scratch_shapes=[pltpu.VMEM((m_rows, s_len + c_len), F32)],
        compiler_params=_params(("arbitrary", "arbitrary", "arbitrary"),
                                4 * (s_len + c_len) * hd * 2 + 4 * tq * gw * 2
                                + m_rows * (s_len + c_len) * 4 + 6 * m_rows * kc * 4),
        name="attention",
    )(qkv, qkv, qkv, qkv, qkv)


def _moe_kernel(tm, f, te_ref, tv_ref, src_ref, h_hbm, gw_ref, w13_ref, w2_ref, y_ref, xbuf, sem):
    del te_ref
    i = pl.program_id(0)
    nt = pl.num_programs(0)
    slot = i % 2

    def row_copy(row, r, sl):
        return pltpu.make_async_copy(h_hbm.at[pl.ds(row, 1)], xbuf.at[sl, pl.ds(r, 1)], sem.at[sl])

    def issue(t, sl):
        def body(r, _):
            row_copy(src_ref[t * tm + r], r, sl).start()
            return 0
        lax.fori_loop(0, tm, body, 0, unroll=8)

    @pl.when(jnp.logical_and(i == 0, tv_ref[0] == 1))
    def _():
        issue(0, 0)

    @pl.when(jnp.logical_and(i + 1 < nt, tv_ref[jnp.minimum(i + 1, nt - 1)] == 1))
    def _():
        issue(i + 1, 1 - slot)

    @pl.when(tv_ref[i] == 1)
    def _():
        pltpu.make_async_copy(h_hbm.at[pl.ds(0, tm)], xbuf.at[slot], sem.at[slot]).wait()
        x = xbuf[slot].astype(BF16)
        h13 = jnp.dot(x, w13_ref[...], preferred_element_type=F32)
        h1 = h13[:, :f]
        act = (h1 * _sigmoid(h1)) * h13[:, f:] * gw_ref[...]
        y_ref[...] = jnp.dot(act.astype(BF16), w2_ref[...], preferred_element_type=F32)

    @pl.when(tv_ref[i] == 0)
    def _():
        y_ref[...] = jnp.zeros_like(y_ref)


def moe_experts(h, tile_expert, tile_valid, src_rows, gate_w, w13, w2, *, tm):
    d = h.shape[1]
    p = src_rows.shape[0]
    f = w2.shape[1]
    nt = p // tm
    grid_spec = pltpu.PrefetchScalarGridSpec(
        num_scalar_prefetch=3,
        grid=(nt,),
        in_specs=[
            pl.BlockSpec(memory_space=pl.ANY),
            pl.BlockSpec((tm, 1), lambda i, te, tv, src: (i, 0)),
            pl.BlockSpec((None, d, 2 * f), lambda i, te, tv, src: (te[i], 0, 0)),
            pl.BlockSpec((None, f, d), lambda i, te, tv, src: (te[i], 0, 0)),
        ],
        out_specs=pl.BlockSpec((tm, d), lambda i, te, tv, src: (i, 0)),
        scratch_shapes=[pltpu.VMEM((2, tm, d), F32), pltpu.SemaphoreType.DMA((2,))],
    )
    return pl.pallas_call(
        functools.partial(_moe_kernel, tm, f),
        grid_spec=grid_spec,
        out_shape=jax.ShapeDtypeStruct((p, d), F32),
        compiler_params=_params(("arbitrary",),
                                2 * tm * d * 4 + 2 * (d * 2 * f * 2 + f * d * 2 + tm * d * 4)
                                + tm * d * 6 + tm * 2 * f * 8),
        name="moe_experts",
    )(tile_expert, tile_valid, src_rows, h, gate_w, w13, w2)


def _combine_kernel(tm, pos_ref, x_ref, g_ref, y_hbm, o_ref, ybuf, sem):
    i = pl.program_id(0)
    nt = pl.num_programs(0)
    slot = i % 2

    def issue(t, sl):
        def body(r, _):
            for k in range(2):
                row = pos_ref[2 * (t * tm + r) + k]
                pltpu.make_async_copy(y_hbm.at[pl.ds(row, 1)], ybuf.at[sl, k, pl.ds(r, 1)],
                                      sem.at[sl]).start()
            return 0
        lax.fori_loop(0, tm, body, 0, unroll=4)

    @pl.when(i == 0)
    def _():
        issue(0, 0)

    @pl.when(i + 1 < nt)
    def _():
        issue(i + 1, 1 - slot)

    for k in range(2):
        pltpu.make_async_copy(y_hbm.at[pl.ds(0, tm)], ybuf.at[slot, k], sem.at[slot]).wait()
    o_ref[...] = x_ref[...] + g_ref[...] * (ybuf[slot, 0] + ybuf[slot, 1])


def moe_combine(x, gate, y, pos, *, s_len, n_batch, tm=128):
    r, d = x.shape
    cls = _cls_of_tile(tm, s_len, s_len * n_batch, n_batch)
    grid_spec = pltpu.PrefetchScalarGridSpec(
        num_scalar_prefetch=1,
        grid=(r // tm,),
        in_specs=[
            pl.BlockSpec((tm, d), lambda i, pos: (i, 0)),
            pl.BlockSpec((None, 1, d), lambda i, pos: (cls(i), 0, 0)),
            pl.BlockSpec(memory_space=pl.ANY),
        ],
        out_specs=pl.BlockSpec((tm, d), lambda i, pos: (i, 0)),
        scratch_shapes=[pltpu.VMEM((2, 2, tm, d), F32), pltpu.SemaphoreType.DMA((2,))],
    )
    return pl.pallas_call(
        functools.partial(_combine_kernel, tm),
        grid_spec=grid_spec,
        out_shape=jax.ShapeDtypeStruct((r, d), F32),
        compiler_params=_params(("arbitrary",), 4 * tm * d * 4 + 4 * tm * d * 4 + 2 * tm * d * 4),
        name="moe_combine",
    )(pos, x, gate, y)


def _route_plan(rinfo, n_experts, tm):
    r = rinfo.shape[0]
    e = rinfo[:, 0:2].astype(jnp.int32).reshape(-1)
    wgt = rinfo[:, 2:4].reshape(-1)
    n_assign = 2 * r
    onehot = (e[:, None] == jnp.arange(n_experts, dtype=jnp.int32)[None, :]).astype(jnp.int32)
    csum = jnp.cumsum(onehot, axis=0)
    rank = jnp.sum(csum * onehot, axis=1) - 1
    counts = csum[-1]
    ntile = (counts + tm - 1) // tm
    tend = jnp.cumsum(ntile)
    tstart = tend - ntile
    pos = tstart[e] * tm + rank
    n_tiles = n_assign // tm + n_experts
    p = n_tiles * tm
    src = jnp.zeros((p,), jnp.int32).at[pos].set(jnp.arange(n_assign, dtype=jnp.int32) // 2)
    gw = jnp.zeros((p,), F32).at[pos].set(wgt)
    tid = jnp.arange(n_tiles, dtype=jnp.int32)
    tile_valid = (tid < tend[-1]).astype(jnp.int32)
    tile_expert = jnp.minimum(jnp.sum((tid[:, None] >= tend[None, :]).astype(jnp.int32), axis=1),
                              n_experts - 1)
    last = jnp.maximum(tend[-1] - 1, 0)
    tile_expert = jnp.where(tile_valid == 1, tile_expert, tile_expert[last])
    return pos, src, gw.reshape(p, 1), tile_expert, tile_valid


def _final_norm_kernel(x_ref, g_ref, o_ref):
    x = x_ref[...]
    ms = jnp.mean(x * x, axis=-1, keepdims=True)
    o_ref[...] = x * lax.rsqrt(ms + EPS) * g_ref[...]


def final_norm(x, g, n_rows, tm=256):
    d = x.shape[1]
    return pl.pallas_call(
        _final_norm_kernel,
        grid=(n_rows // tm,),
        in_specs=[pl.BlockSpec((tm, d), lambda i: (i, 0)), pl.BlockSpec((1, d), lambda i: (0, 0))],
        out_specs=pl.BlockSpec((tm, d), lambda i: (i, 0)),
        out_shape=jax.ShapeDtypeStruct((n_rows, d), F32),
        compiler_params=_params(("arbitrary",), 4 * tm * d * 4),
        name="final_norm",
    )(x, g.reshape(1, d))


def _rope_tables(s_len, hd, tm):
    rows = s_len // ROPE_GRID_W
    t_row = jnp.repeat(jnp.arange(rows), ROPE_GRID_W).astype(F32)
    t_col = jnp.tile(jnp.arange(ROPE_GRID_W), rows).astype(F32)
    n_f = hd // 4
    inv = ROPE_THETA ** (-jnp.arange(n_f, dtype=F32) / n_f)
    ang = jnp.concatenate([t_row[:, None] * inv, t_col[:, None] * inv], axis=-1)
    cos, sin = jnp.cos(ang), jnp.sin(ang)
    cos2 = jnp.concatenate([cos, cos], axis=-1)
    sin2 = jnp.concatenate([-sin, sin], axis=-1)
    cos2 = jnp.concatenate([cos2, jnp.ones((tm, hd), F32)], axis=0)
    sin2 = jnp.concatenate([sin2, jnp.zeros((tm, hd), F32)], axis=0)
    return cos2, sin2


def kernel(x, c, ctx, c_ctx, ada_w, ada_b, norm_mix, norm_ffn, final_norm_g, lru_w_in, lru_conv_w,
           lru_conv_b, lru_w_a, lru_b_a, lru_w_i, lru_b_i, lru_lam, lru_w_out, attn_w_qkv, attn_q_norm,
           attn_k_norm, attn_w_o, moe_w_rg, moe_b_rg, moe_w_re, moe_b_re, moe_w1, moe_w3, moe_w2):
    n_batch, s_len, d = x.shape
    c_len = ctx.shape[1]
    depth = ada_w.shape[0]
    n_ada = ada_w.shape[2] // d
    d_rnn = lru_w_in.shape[2] // 2
    hd = attn_q_norm.shape[1]
    n_heads = d // hd
    n_kv = (attn_w_qkv.shape[2] // hd - n_heads) // 2
    n_groups = moe_w_rg.shape[2]
    n_experts = moe_w_re.shape[2]
    per_group = n_experts // n_groups
    n_lat = n_batch * s_len
    tm = 512 if (s_len % 512 == 0 and (n_batch * c_len) % 512 == 0) else 256
    moe_tm = 256
    geo = dict(s_len=s_len, n_batch=n_batch)
    assert n_batch + 1 <= NCLS_PAD and n_groups + n_experts <= LANES

    xs = jnp.concatenate([x.reshape(n_lat, d), ctx.reshape(n_batch * c_len, d)], axis=0)

    cond = jnp.zeros((NCLS_PAD, d), F32).at[:n_batch].set(c).at[n_batch].set(c_ctx)
    mods = ada_all(cond, ada_w, ada_b).reshape(depth, NCLS_PAD, n_ada, d)
    cos_tbl, sin_tbl = _rope_tables(s_len, hd, tm)

    def cls_vec(v):
        return v.reshape(NCLS_PAD, 1, d)

    n_mixers = 2
    for l in range(depth):
        sh_m, sc_m, g_m, sh_f, sc_f, g_f = [mods[l, :, j] for j in range(n_ada)]
        j = l // n_mixers
        h = modulate(xs, cls_vec(norm_mix[l] * (1.0 + sc_m)), cls_vec(sh_m), **geo)
        if l % n_mixers == 0:
            w_in = lru_w_in[j].astype(BF16)
            yg = matmul(h, w_in[:, :d_rnn], out_dtype=BF16, act="gelu", tm=tm)
            xr = matmul(h, w_in[:, d_rnn:], out_dtype=F32, tm=tm)
            z = lru_mix(xr, yg, lru_conv_w[j], lru_conv_b[j], lru_w_a[j].astype(BF16),
                        lru_w_i[j].astype(BF16), lru_b_a[j], lru_b_i[j], lru_lam[j],
                        c_len=c_len, **geo)
            xs = matmul_resid(z, lru_w_out[j].astype(BF16), xs, cls_vec(g_m), tm=tm, **geo)
        else:
            scale = hd ** -0.5
            gain = jnp.concatenate([jnp.tile(attn_q_norm[j] * scale, n_heads),
                                    jnp.tile(attn_k_norm[j], n_kv),
                                    jnp.ones((n_kv * hd,), F32)]).reshape(1, -1)
            qkv = matmul_qkv(h, attn_w_qkv[j].astype(BF16), gain, cos_tbl, sin_tbl,
                             n_norm_cols=(n_heads + n_kv) * hd, hd=hd, tm=tm, **geo)
            o = attention(qkv, n_heads=n_heads, n_kv=n_kv, hd=hd, c_len=c_len, d_out=d, **geo)
            xs = matmul_resid(o, attn_w_o[j].astype(BF16), xs, cls_vec(g_m), tm=tm, **geo)

        wr = jnp.zeros((d, LANES), F32).at[:, :n_groups].set(moe_w_rg[l])
        wr = wr.at[:, n_groups:n_groups + n_experts].set(moe_w_re[l])
        br = jnp.zeros((1, LANES), F32).at[0, :n_groups].set(moe_b_rg[l])
        br = br.at[0, n_groups:n_groups + n_experts].set(moe_b_re[l])
        hf, rinfo = modulate_router(xs, cls_vec(norm_ffn[l] * (1.0 + sc_f)), cls_vec(sh_f), wr, br,
                                    n_groups=n_groups, per_group=per_group, **geo)
        pos, src, gw, tile_expert, tile_valid = _route_plan(rinfo, n_experts, moe_tm)
        w13 = jnp.concatenate([moe_w1[l], moe_w3[l]], axis=-1).astype(BF16)
        y = moe_experts(hf, tile_expert, tile_valid, src, gw, w13, moe_w2[l].astype(BF16), tm=moe_tm)
        xs = moe_combine(xs, cls_vec(g_f), y, pos, **geo)

    out = final_norm(xs, final_norm_g, n_lat)
    return out.reshape(n_batch, s_len, d)
```

```python
import functools
import math

import jax
import jax.numpy as jnp
from jax import lax
from jax.experimental import pallas as pl
from jax.experimental.pallas import tpu as pltpu

EPS = 1e-6
LRU_C = 8.0
ROPE_THETA = 10000.0
ROPE_GRID_W = 64
LANES = 128
SUBLANES = 8
NCLS_PAD = 8
V7X_VMEM_CAP = 56 * 1024 * 1024
BF16 = jnp.bfloat16
F32 = jnp.float32


def _vmem_limit(nbytes):
    return int(min(V7X_VMEM_CAP, max(16 * 1024 * 1024, nbytes * 5 // 4 + (4 << 20))))


def _params(sem, nbytes):
    return pltpu.CompilerParams(dimension_semantics=sem, vmem_limit_bytes=_vmem_limit(nbytes))


def _sigmoid(x):
    return 1.0 / (1.0 + jnp.exp(-x))


def _gelu_tanh(x):
    return 0.5 * x * (1.0 + jnp.tanh(math.sqrt(2.0 / math.pi) * (x + 0.044715 * (x * x * x))))


def _ada_kernel(c_ref, w_ref, b_ref, o_ref):
    c = c_ref[...]
    s = (c * _sigmoid(c)).astype(BF16)
    w = w_ref[...].astype(BF16)
    o_ref[...] = jnp.dot(s, w, preferred_element_type=F32) + b_ref[...]


def ada_all(cond, ada_w, ada_b):
    depth, d, n = ada_w.shape
    tn = 512
    return pl.pallas_call(
        _ada_kernel,
        grid=(depth, n // tn),
        in_specs=[
            pl.BlockSpec((NCLS_PAD, d), lambda l, j: (0, 0)),
            pl.BlockSpec((None, d, tn), lambda l, j: (l, 0, j)),
            pl.BlockSpec((None, 1, tn), lambda l, j: (l, 0, j)),
        ],
        out_specs=pl.BlockSpec((None, NCLS_PAD, tn), lambda l, j: (l, 0, j)),
        out_shape=jax.ShapeDtypeStruct((depth, NCLS_PAD, n), F32),
        compiler_params=_params(("arbitrary", "arbitrary"), 2 * d * tn * 4 + d * tn * 2),
        name="ada_all",
    )(cond, ada_w, ada_b.reshape(depth, 1, n))


def _modulate(x, gs, sh):
    ms = jnp.mean(x * x, axis=-1, keepdims=True)
    return x * lax.rsqrt(ms + EPS) * gs + sh


def _mod_kernel(x_ref, gs_ref, sh_ref, o_ref):
    o_ref[...] = _modulate(x_ref[...], gs_ref[...], sh_ref[...]).astype(o_ref.dtype)


def _mod_router_kernel(n_groups, per_group, x_ref, gs_ref, sh_ref, wr_ref, br_ref, h_ref, r_ref):
    h = _modulate(x_ref[...], gs_ref[...], sh_ref[...])
    h_ref[...] = h
    logits = jnp.dot(h, wr_ref[...], preferred_element_type=F32,
                     precision=lax.Precision.HIGHEST) + br_ref[...]
    lane = lax.broadcasted_iota(jnp.int32, logits.shape, 1)
    neg = jnp.float32(-jnp.inf)
    big = jnp.int32(1 << 20)
    gl = jnp.where(lane < n_groups, logits, neg)
    gmax = jnp.max(gl, axis=-1, keepdims=True)
    gsum = jnp.sum(jnp.exp(gl - gmax), axis=-1, keepdims=True)
    g_top = 1.0 / gsum
    g_idx = jnp.min(jnp.where(gl == gmax, lane, big), axis=-1, keepdims=True)
    lo = n_groups + per_group * g_idx
    el = jnp.where((lane >= lo) & (lane < lo + per_group), logits, neg)
    m1 = jnp.max(el, axis=-1, keepdims=True)
    i1 = jnp.min(jnp.where(el == m1, lane, big), axis=-1, keepdims=True)
    el2 = jnp.where(lane == i1, neg, el)
    m2 = jnp.max(el2, axis=-1, keepdims=True)
    i2 = jnp.min(jnp.where(el2 == m2, lane, big), axis=-1, keepdims=True)
    t = jnp.exp(m2 - m1)
    w1 = g_top / (1.0 + t)
    w2 = g_top * t / (1.0 + t)
    e1 = (i1 - n_groups).astype(F32)
    e2 = (i2 - n_groups).astype(F32)
    r_ref[...] = jnp.where(lane == 0, e1, jnp.where(lane == 1, e2,
                           jnp.where(lane == 2, w1, jnp.where(lane == 3, w2, 0.0))))


def _cls_of_tile(tm, s_len, n_lat, n_batch):
    def f(i):
        r0 = i * tm
        return jnp.where(r0 < n_lat, r0 // s_len, n_batch)
    return f


def modulate(x, gs, sh, *, s_len, n_batch, tm=256):
    r, d = x.shape
    cls = _cls_of_tile(tm, s_len, s_len * n_batch, n_batch)
    return pl.pallas_call(
        _mod_kernel,
        grid=(r // tm,),
        in_specs=[
            pl.BlockSpec((tm, d), lambda i: (i, 0)),
            pl.BlockSpec((None, 1, d), lambda i: (cls(i), 0, 0)),
            pl.BlockSpec((None, 1, d), lambda i: (cls(i), 0, 0)),
        ],
        out_specs=pl.BlockSpec((tm, d), lambda i: (i, 0)),
        out_shape=jax.ShapeDtypeStruct((r, d), BF16),
        compiler_params=_params(("arbitrary",), 2 * tm * d * 6),
        name="modulate",
    )(x, gs, sh)


def modulate_router(x, gs, sh, wr, br, *, n_groups, per_group, s_len, n_batch, tm=256):
    r, d = x.shape
    cls = _cls_of_tile(tm, s_len, s_len * n_batch, n_batch)
    return pl.pallas_call(
        functools.partial(_mod_router_kernel, n_groups, per_group),
        grid=(r // tm,),
        in_specs=[
            pl.BlockSpec((tm, d), lambda i: (i, 0)),
            pl.BlockSpec((None, 1, d), lambda i: (cls(i), 0, 0)),
            pl.BlockSpec((None, 1, d), lambda i: (cls(i), 0, 0)),
            pl.BlockSpec((d, LANES), lambda i: (0, 0)),
            pl.BlockSpec((1, LANES), lambda i: (0, 0)),
        ],
        out_specs=[
            pl.BlockSpec((tm, d), lambda i: (i, 0)),
            pl.BlockSpec((tm, LANES), lambda i: (i, 0)),
        ],
        out_shape=[jax.ShapeDtypeStruct((r, d), F32), jax.ShapeDtypeStruct((r, LANES), F32)],
        compiler_params=_params(("arbitrary",), 2 * tm * d * 8 + 2 * d * LANES * 4 + 4 * tm * d * 4),
        name="modulate_router",
    )(x, gs, sh, wr, br)


def _cast_weight_once(w_ref, wbf_ref):
    @pl.when(pl.program_id(1) == 0)
    def _():
        wbf_ref[...] = w_ref[...].astype(BF16)


def _mm_plain_kernel(act, a_ref, w_ref, o_ref, wbf_ref):
    _cast_weight_once(w_ref, wbf_ref)
    acc = jnp.dot(a_ref[...], wbf_ref[...], preferred_element_type=F32)
    if act == "gelu":
        acc = _gelu_tanh(acc)
    o_ref[...] = acc.astype(o_ref.dtype)


def _mm_resid_kernel(n_main_tiles, a_ref, a2_ref, w_ref, x_ref, g_ref, o_ref, wbf_ref):
    _cast_weight_once(w_ref, wbf_ref)
    i = pl.program_id(1)

    def emit(src_ref):
        acc = jnp.dot(src_ref[...], wbf_ref[...], preferred_element_type=F32)
        o_ref[...] = x_ref[...] + g_ref[...] * acc

    @pl.when(i < n_main_tiles)
    def _():
        emit(a_ref)

    @pl.when(i >= n_main_tiles)
    def _():
        emit(a2_ref)


def _mm_qkv_kernel(n_norm_tiles, hd, a_ref, w_ref, gain_ref, cos_ref, sin_ref, o_ref, wbf_ref):
    _cast_weight_once(w_ref, wbf_ref)
    acc = jnp.dot(a_ref[...], wbf_ref[...], preferred_element_type=F32)
    j = pl.program_id(0)

    @pl.when(j < n_norm_tiles)
    def _():
        cos = cos_ref[...]
        sin = sin_ref[...]
        for h in range(acc.shape[1] // hd):
            xh = acc[:, h * hd:(h + 1) * hd]
            ms = jnp.mean(xh * xh, axis=-1, keepdims=True)
            y = xh * lax.rsqrt(ms + EPS) * gain_ref[:, h * hd:(h + 1) * hd]
            y = y * cos + pltpu.roll(y, hd // 2, 1) * sin
            o_ref[:, h * hd:(h + 1) * hd] = y.astype(o_ref.dtype)

    @pl.when(j >= n_norm_tiles)
    def _():
        o_ref[...] = acc.astype(o_ref.dtype)


def _w_spec(k, tn, layer, col0):
    assert col0 % tn == 0
    return pl.BlockSpec((None, k, tn), lambda j, i: (layer, 0, col0 // tn + j))


def _mm_vmem(tm, k, tn, out_bytes, extra=0):
    return 2 * (tm * k * 2 + k * tn * 4 + tm * tn * out_bytes) + k * tn * 2 + 2 * tm * tn * 4 + extra


def matmul(a, w, layer, *, n_cols, col0=0, out_dtype, act=None, tm=512, tn=512):
    r, k = a.shape
    tn = min(tn, n_cols)
    assert r % tm == 0 and n_cols % tn == 0
    ob = jnp.dtype(out_dtype).itemsize
    return pl.pallas_call(
        functools.partial(_mm_plain_kernel, act),
        grid=(n_cols // tn, r // tm),
        in_specs=[pl.BlockSpec((tm, k), lambda j, i: (i, 0)), _w_spec(k, tn, layer, col0)],
        out_specs=pl.BlockSpec((tm, tn), lambda j, i: (i, j)),
        out_shape=jax.ShapeDtypeStruct((r, n_cols), out_dtype),
        scratch_shapes=[pltpu.VMEM((k, tn), BF16)],
        compiler_params=_params(("arbitrary", "arbitrary"), _mm_vmem(tm, k, tn, ob)),
        name="matmul_" + (act or "plain"),
    )(a, w)


def matmul_resid(a, w, layer, xres, gate, *, s_len, n_batch, a_tail=None, tm=512, tn=512):
    r, n = xres.shape
    k = a.shape[1]
    tn = min(tn, n)
    assert r % tm == 0 and n % tn == 0 and a.shape[0] % tm == 0
    if a_tail is None:
        a_tail = a
    n_main = a.shape[0] // tm
    n_tail = a_tail.shape[0] // tm
    cls = _cls_of_tile(tm, s_len, s_len * n_batch, n_batch)
    return pl.pallas_call(
        functools.partial(_mm_resid_kernel, n_main),
        grid=(n // tn, r // tm),
        in_specs=[pl.BlockSpec((tm, k), lambda j, i: (jnp.minimum(i, n_main - 1), 0)),
                  pl.BlockSpec((tm, k), lambda j, i: (jnp.clip(i - n_main, 0, n_tail - 1), 0)),
                  _w_spec(k, tn, layer, 0),
                  pl.BlockSpec((tm, tn), lambda j, i: (i, j)),
                  pl.BlockSpec((None, 1, tn), lambda j, i: (cls(i), 0, j))],
        out_specs=pl.BlockSpec((tm, tn), lambda j, i: (i, j)),
        out_shape=jax.ShapeDtypeStruct((r, n), F32),
        scratch_shapes=[pltpu.VMEM((k, tn), BF16)],
        compiler_params=_params(("arbitrary", "arbitrary"),
                                _mm_vmem(tm, k, tn, 4, 2 * tm * k * 2 + 2 * tm * tn * 4)),
        name="matmul_resid",
    )(a, a_tail, w, xres, gate)


def matmul_qkv(a, w, layer, gain, cos_tbl, sin_tbl, *, n_norm_cols, hd, s_len, n_batch, tm=512, tn=512):
    r, k = a.shape
    n = w.shape[2]
    while n_norm_cols % tn or n % tn:
        tn //= 2
    assert tn % hd == 0 and s_len % tm == 0
    n_lat = s_len * n_batch
    pos_blocks = s_len // tm

    def pos(i):
        return jnp.where(i * tm < n_lat, i % pos_blocks, pos_blocks)

    return pl.pallas_call(
        functools.partial(_mm_qkv_kernel, n_norm_cols // tn, hd),
        grid=(n // tn, r // tm),
        in_specs=[pl.BlockSpec((tm, k), lambda j, i: (i, 0)),
                  _w_spec(k, tn, layer, 0),
                  pl.BlockSpec((1, tn), lambda j, i: (0, j)),
                  pl.BlockSpec((tm, hd), lambda j, i: (pos(i), 0)),
                  pl.BlockSpec((tm, hd), lambda j, i: (pos(i), 0))],
        out_specs=pl.BlockSpec((tm, tn), lambda j, i: (i, j)),
        out_shape=jax.ShapeDtypeStruct((r, n), BF16),
        scratch_shapes=[pltpu.VMEM((k, tn), BF16)],
        compiler_params=_params(("arbitrary", "arbitrary"),
                                _mm_vmem(tm, k, tn, 2, 4 * tm * hd * 4 + tm * tn * 4)),
        name="matmul_qkv",
    )(a, w, gain, cos_tbl, sin_tbl)


def _seg_geometry(t_len):
    ls = t_len // SUBLANES
    assert ls % SUBLANES == 0
    return ls, ls + SUBLANES


def _lru_kernel(tc, xl_ref, xc_ref, yl_ref, yc_ref, cw_ref, cb_ref, wa_ref, wi_ref, ba_ref, bi_ref,
                lam_ref, ol_ref, oc_ref, cvl_ref, cvc_ref, accl_ref, accc_ref,
                al_ref, ul_ref, ac_ref, uc_ref):
    w = xl_ref.shape[1]
    cw = cw_ref[...]
    cb = cb_ref[...]

    def conv_seq(x_ref, cv_ref):
        t_len = x_ref.shape[0]
        nchunk = t_len // tc

        def body(ci, _):
            t0 = pl.multiple_of(ci * tc, tc)
            main = x_ref[pl.ds(t0, tc), :]
            p0 = pl.multiple_of(jnp.maximum(t0 - 8, 0), 8)
            n0 = pl.multiple_of(jnp.minimum(t0 + tc, t_len - 8), 8)
            prev8 = jnp.where(ci > 0, x_ref[pl.ds(p0, 8), :], 0.0)
            next8 = jnp.where(ci < nchunk - 1, x_ref[pl.ds(n0, 8), :], 0.0)
            ext = jnp.concatenate([prev8, main, next8], axis=0)
            ne = tc + 16
            xm1 = pltpu.roll(ext, 1, 0)[8:8 + tc]
            xp1 = pltpu.roll(ext, ne - 1, 0)[8:8 + tc]
            xp2 = pltpu.roll(ext, ne - 2, 0)[8:8 + tc]
            cv_ref[pl.ds(t0, tc), :] = (cw[0:1] * xm1 + cw[1:2] * main + cw[2:3] * xp1
                                        + cw[3:4] * xp2 + cb)
            return 0

        lax.fori_loop(0, nchunk, body, 0)

    conv_seq(xl_ref, cvl_ref)
    conv_seq(xc_ref, cvc_ref)

    def run_seq(d, cv_ref, acc_ref, a_ref, u_ref, carry_in):
        reverse = d == 1
        t_len = cv_ref.shape[0]
        ls, stride = _seg_geometry(t_len)
        piece = min(tc, ls)
        n_lt = w // LANES
        wa, wi, ba, bi, lam = wa_ref[d], wi_ref[d], ba_ref[d], bi_ref[d], lam_ref[d]
        nsp = -LRU_C * (jnp.maximum(-lam, 0.0) + jnp.log(1.0 + jnp.exp(-jnp.abs(lam))))

        def seg_row(t):
            return pl.multiple_of(t + SUBLANES * (t // ls), SUBLANES)

        def coeff_body(ci, _):
            t0 = pl.multiple_of(ci * tc, tc)
            x = cv_ref[pl.ds(t0, tc), :]
            xb = x.astype(BF16)
            r = _sigmoid(jnp.dot(xb, wa, preferred_element_type=F32) + ba)
            ig = _sigmoid(jnp.dot(xb, wi, preferred_element_type=F32) + bi)
            a = jnp.exp(r * nsp)
            u = jnp.sqrt(1.0 - a * a) * (ig * x)
            for p in range(tc // piece):
                row = seg_row(t0 + p * piece)
                for lt in range(n_lt):
                    lanes = slice(lt * LANES, (lt + 1) * LANES)
                    a_ref[lt, pl.ds(row, piece), :] = a[p * piece:(p + 1) * piece, lanes]
                    u_ref[lt, pl.ds(row, piece), :] = u[p * piece:(p + 1) * piece, lanes]
            return 0

        lax.fori_loop(0, t_len // tc, coeff_body, 0)

        def scan_body(k, carry):
            r = (ls - 1 - k) if reverse else k
            idx = pl.ds(r, SUBLANES, stride=stride)
            out = []
            for lt in range(n_lt):
                h, p = carry[lt]
                a = a_ref[lt, idx, :]
                h = a * h + u_ref[lt, idx, :]
                p = a * p
                u_ref[lt, idx, :] = h
                a_ref[lt, idx, :] = p
                out.append((h, p))
            return tuple(out)

        init = tuple((jnp.zeros((SUBLANES, LANES), F32), jnp.ones((SUBLANES, LANES), F32))
                     for _ in range(n_lt))
        ends = lax.fori_loop(0, ls, scan_body, init, unroll=8)
        h_end = jnp.concatenate([e[0] for e in ends], axis=1)
        p_end = jnp.concatenate([e[1] for e in ends], axis=1)

        c = carry_in
        seg_in = [None] * SUBLANES
        for s in (range(SUBLANES - 1, -1, -1) if reverse else range(SUBLANES)):
            seg_in[s] = c
            c = p_end[s:s + 1] * c + h_end[s:s + 1]
        for s in range(SUBLANES):
            for q in range(ls // piece):
                src = s * stride + q * piece
                dst = s * ls + q * piece
                for lt in range(n_lt):
                    lanes = slice(lt * LANES, (lt + 1) * LANES)
                    h = (u_ref[lt, src:src + piece, :]
                         + a_ref[lt, src:src + piece, :] * seg_in[s][:, lanes])
                    if d == 0:
                        acc_ref[dst:dst + piece, lanes] = h
                    else:
                        acc_ref[dst:dst + piece, lanes] = acc_ref[dst:dst + piece, lanes] + h
        return c

    for d in range(2):
        carry = run_seq(d, cvc_ref, accc_ref, ac_ref, uc_ref, jnp.zeros((1, w), F32))
        run_seq(d, cvl_ref, accl_ref, al_ref, ul_ref, carry)

    def finish(acc_ref, y_ref, o_ref):
        nchunk = acc_ref.shape[0] // tc

        def body(ci, _):
            t0 = pl.multiple_of(ci * tc, tc)
            o_ref[pl.ds(t0, tc), :] = (acc_ref[pl.ds(t0, tc), :]
                                       * y_ref[pl.ds(t0, tc), :].astype(F32)).astype(o_ref.dtype)
            return 0

        lax.fori_loop(0, nchunk, body, 0)

    finish(accl_ref, yl_ref, ol_ref)
    finish(accc_ref, yc_ref, oc_ref)


def lru_mix(xr, yg, layer, conv_w, conv_b, w_a, w_i, b_a, b_i, lam, *, s_len, c_len, n_batch, tc=256):
    r, rn = xr.shape
    nblk, bw = w_a.shape[2], w_a.shape[3]
    assert s_len % tc == 0 and c_len % tc == 0 and (n_batch * s_len) % c_len == 0
    ctx0 = n_batch * s_len // c_len
    lat = lambda b, k: (b, k)
    ctx = lambda b, k: (ctx0 + b, k)
    vec = lambda b, k: (layer, 0, 0, k)
    seg_rows = lambda t: SUBLANES * _seg_geometry(t)[1]
    seg_buf = lambda t: pltpu.VMEM((bw // LANES, seg_rows(t), LANES), F32)
    n_dir = w_a.shape[1]
    return pl.pallas_call(
        functools.partial(_lru_kernel, tc),
        grid=(n_batch, nblk),
        in_specs=[
            pl.BlockSpec((s_len, bw), lat), pl.BlockSpec((c_len, bw), ctx),
            pl.BlockSpec((s_len, bw), lat), pl.BlockSpec((c_len, bw), ctx),
            pl.BlockSpec((None, conv_w.shape[1], bw), lambda b, k: (layer, 0, k)),
            pl.BlockSpec((None, 1, bw), lambda b, k: (layer, 0, k)),
            pl.BlockSpec((None, n_dir, None, bw, bw), lambda b, k: (layer, 0, k, 0, 0)),
            pl.BlockSpec((None, n_dir, None, bw, bw), lambda b, k: (layer, 0, k, 0, 0)),
            pl.BlockSpec((None, n_dir, 1, bw), vec), pl.BlockSpec((None, n_dir, 1, bw), vec),
            pl.BlockSpec((None, n_dir, 1, bw), vec),
        ],
        out_specs=[pl.BlockSpec((s_len, bw), lambda b, k: (b, k)),
                   pl.BlockSpec((c_len, bw), lambda b, k: (b, k))],
        out_shape=[jax.ShapeDtypeStruct((n_batch * s_len, rn), BF16),
                   jax.ShapeDtypeStruct((n_batch * c_len, rn), BF16)],
        scratch_shapes=[pltpu.VMEM((s_len, bw), F32), pltpu.VMEM((c_len, bw), F32),
                        pltpu.VMEM((s_len, bw), F32), pltpu.VMEM((c_len, bw), F32),
                        seg_buf(s_len), seg_buf(s_len), seg_buf(c_len), seg_buf(c_len)],
        compiler_params=_params(("arbitrary", "arbitrary"),
                                (s_len + c_len) * bw * (2 * 4 + 2 * 2 + 2 * 2 + 8)
                                + 2 * (seg_rows(s_len) + seg_rows(c_len)) * bw * 4
                                + 8 * bw * bw + 24 * tc * bw * 4),
        name="lru_mix",
    )(xr, xr, yg, yg, conv_w, conv_b.reshape(conv_b.shape[0], 1, rn), w_a.astype(BF16), w_i.astype(BF16),
      b_a.reshape(-1, n_dir, 1, rn), b_i.reshape(-1, n_dir, 1, rn), lam.reshape(-1, n_dir, 1, rn))


_DN_NT = (((1,), (1,)), ((), ()))


def _split_heads(q, groups, hd):
    return jnp.concatenate([q[:, g * hd:(g + 1) * hd] for g in range(groups)], axis=0)


def _attn_pass1(q4, k_refs, s_ref, m_ref, kc):
    m_rows = q4.shape[0]
    mx = jnp.full((m_rows, LANES), -jnp.inf, F32)
    off = 0
    for k_ref in k_refs:
        for c0 in range(0, k_ref.shape[0], kc):
            w = min(kc, k_ref.shape[0] - c0)
            s = lax.dot_general(q4, k_ref[c0:c0 + w, :], _DN_NT, preferred_element_type=F32)
            s_ref[:, off:off + w] = s
            for t in range(w // LANES):
                mx = jnp.maximum(mx, s[:, t * LANES:(t + 1) * LANES])
            off += w
    m_ref[...] = jnp.broadcast_to(jnp.max(mx, axis=-1, keepdims=True), (m_rows, LANES))


def _attn_pass2(v_refs, s_ref, m_ref, kc):
    m_rows = s_ref.shape[0]
    hd = v_refs[0].shape[1]
    mrow = m_ref[...]
    lsum = jnp.zeros((m_rows, LANES), F32)
    o = jnp.zeros((m_rows, hd), F32)
    off = 0
    for v_ref in v_refs:
        for c0 in range(0, v_ref.shape[0], kc):
            w = min(kc, v_ref.shape[0] - c0)
            ps = []
            for t in range(w // LANES):
                p = jnp.exp2(s_ref[:, off + t * LANES:off + (t + 1) * LANES] - mrow)
                lsum = lsum + p
                ps.append(p.astype(BF16))
            p_all = jnp.concatenate(ps, axis=1) if len(ps) > 1 else ps[0]
            o = o + jnp.dot(p_all, v_ref[c0:c0 + w, :], preferred_element_type=F32)
            off += w
    return o / jnp.sum(lsum, axis=-1, keepdims=True)


def _emit_heads(o, o_ref, row0, tq, groups, hd):
    for g in range(groups):
        o_ref[row0:row0 + tq, g * hd:(g + 1) * hd] = o[g * tq:(g + 1) * tq].astype(o_ref.dtype)


def _attn_lat_kernel(tq, groups, hd, kc, qc_ref, qn_ref, kl_ref, vl_ref, kcx_ref, vcx_ref, o_ref,
                     s0_ref, s1_ref, m0_ref, m1_ref):
    keys = [kcx_ref, kl_ref]
    vals = [vcx_ref, vl_ref]

    @pl.when(pl.program_id(2) == 0)
    def _():
        _attn_pass1(_split_heads(qc_ref[0:tq, :], groups, hd), keys, s0_ref, m0_ref, kc)

    _attn_pass1(_split_heads(qc_ref[tq:2 * tq, :], groups, hd), keys, s1_ref, m1_ref, kc)
    _emit_heads(_attn_pass2(vals, s0_ref, m0_ref, kc), o_ref, 0, tq, groups, hd)
    _attn_pass1(_split_heads(qn_ref[0:tq, :], groups, hd), keys, s0_ref, m0_ref, kc)
    _emit_heads(_attn_pass2(vals, s1_ref, m1_ref, kc), o_ref, tq, tq, groups, hd)


def _attn_ctx_kernel(tq, groups, hd, kc, q_ref, k_ref, v_ref, o_ref, s_ref, m_ref):
    _attn_pass1(_split_heads(q_ref[...], groups, hd), [k_ref], s_ref, m_ref, kc)
    _emit_heads(_attn_pass2([v_ref], s_ref, m_ref, kc), o_ref, 0, tq, groups, hd)


def attention(qkv, *, n_heads, n_kv, hd, s_len, c_len, n_batch, d_out, tq=128, kc=512):
    groups = n_heads // n_kv
    gw = groups * hd
    m_rows = groups * tq
    assert s_len % (2 * tq) == 0 and c_len % tq == 0 and (n_batch * s_len) % c_len == 0
    n_pair = s_len // (2 * tq)
    ctx_k0 = n_batch * s_len // c_len
    kv_bytes = 4 * (s_len + c_len) * hd * 2
    o_lat = pl.pallas_call(
        functools.partial(_attn_lat_kernel, tq, groups, hd, kc),
        grid=(n_batch, n_kv, n_pair),
        in_specs=[
            pl.BlockSpec((2 * tq, gw), lambda b, kv, i: (b * n_pair + i, kv)),
            pl.BlockSpec((2 * tq, gw), lambda b, kv, i: (b * n_pair + jnp.minimum(i + 1, n_pair - 1), kv)),
            pl.BlockSpec((s_len, hd), lambda b, kv, i: (b, n_heads + kv)),
            pl.BlockSpec((s_len, hd), lambda b, kv, i: (b, n_heads + n_kv + kv)),
            pl.BlockSpec((c_len, hd), lambda b, kv, i: (ctx_k0 + b, n_heads + kv)),
            pl.BlockSpec((c_len, hd), lambda b, kv, i: (ctx_k0 + b, n_heads + n_kv + kv)),
        ],
        out_specs=pl.BlockSpec((2 * tq, gw), lambda b, kv, i: (b * n_pair + i, kv)),
        out_shape=jax.ShapeDtypeStruct((n_batch * s_len, d_out), BF16),
        scratch_shapes=[pltpu.VMEM((m_rows, s_len + c_len), F32), pltpu.VMEM((m_rows, s_len + c_len), F32),
                        pltpu.VMEM((m_rows, LANES), F32), pltpu.VMEM((m_rows, LANES), F32)],
        compiler_params=_params(("arbitrary", "arbitrary", "arbitrary"),
                                kv_bytes + 12 * tq * gw * 2 + 2 * m_rows * (s_len + c_len) * 4
                                + 8 * m_rows * kc * 4),
        name="attention_lat",
    )(qkv, qkv, qkv, qkv, qkv, qkv)

    nq_ctx = c_len // tq
    ctx_q0 = n_batch * s_len // tq
    o_ctx = pl.pallas_call(
        functools.partial(_attn_ctx_kernel, tq, groups, hd, kc),
        grid=(n_batch, n_kv, nq_ctx),
        in_specs=[
            pl.BlockSpec((tq, gw), lambda b, kv, i: (ctx_q0 + b * nq_ctx + i, kv)),
            pl.BlockSpec((c_len, hd), lambda b, kv, i: (ctx_k0 + b, n_heads + kv)),
            pl.BlockSpec((c_len, hd), lambda b, kv, i: (ctx_k0 + b, n_heads + n_kv + kv)),
        ],
        out_specs=pl.BlockSpec((tq, gw), lambda b, kv, i: (b * nq_ctx + i, kv)),
        out_shape=jax.ShapeDtypeStruct((n_batch * c_len, d_out), BF16),
        scratch_shapes=[pltpu.VMEM((m_rows, c_len), F32), pltpu.VMEM((m_rows, LANES), F32)],
        compiler_params=_params(("arbitrary", "arbitrary", "arbitrary"),
                                4 * c_len * hd * 2 + 4 * tq * gw * 2 + m_rows * c_len * 4
                                + 8 * m_rows * kc * 4),
        name="attention_ctx",
    )(qkv, qkv, qkv)
    return o_lat, o_ctx


def _moe_kernel(tm, f, te_ref, tv_ref, src_ref, h_hbm, gw_ref, w1_ref, w3_ref, w2_ref, y_ref,
                xbuf, w13b, w2b, sem):
    i = pl.program_id(0)
    nt = pl.num_programs(0)
    slot = i % 2

    def row_copy(row, r, sl):
        return pltpu.make_async_copy(h_hbm.at[pl.ds(row, 1)], xbuf.at[sl, pl.ds(r, 1)], sem.at[sl])

    def issue(t, sl):
        def body(r, _):
            row_copy(src_ref[t * tm + r], r, sl).start()
            return 0
        lax.fori_loop(0, tm, body, 0, unroll=8)

    @pl.when(jnp.logical_and(i == 0, tv_ref[0] == 1))
    def _():
        issue(0, 0)

    @pl.when(jnp.logical_and(i + 1 < nt, tv_ref[jnp.minimum(i + 1, nt - 1)] == 1))
    def _():
        issue(i + 1, 1 - slot)

    new_expert = jnp.logical_or(i == 0, te_ref[i] != te_ref[jnp.maximum(i - 1, 0)])

    @pl.when(jnp.logical_and(new_expert, tv_ref[i] == 1))
    def _():
        w13b[:, :f] = w1_ref[...].astype(BF16)
        w13b[:, f:] = w3_ref[...].astype(BF16)
        w2b[...] = w2_ref[...].astype(BF16)

    @pl.when(tv_ref[i] == 1)
    def _():
        pltpu.make_async_copy(h_hbm.at[pl.ds(0, tm)], xbuf.at[slot], sem.at[slot]).wait()
        x = xbuf[slot].astype(BF16)
        h13 = jnp.dot(x, w13b[...], preferred_element_type=F32)
        h1 = h13[:, :f]
        act = (h1 * _sigmoid(h1)) * h13[:, f:] * gw_ref[...]
        y_ref[...] = jnp.dot(act.astype(BF16), w2b[...], preferred_element_type=F32)

    @pl.when(tv_ref[i] == 0)
    def _():
        y_ref[...] = jnp.zeros_like(y_ref)


def moe_experts(h, tile_expert, tile_valid, src_rows, gate_w, w1, w3, w2, layer, *, tm):
    d = h.shape[1]
    p = src_rows.shape[0]
    f = w2.shape[2]
    nt = p // tm
    single = pl.Buffered(1)
    grid_spec = pltpu.PrefetchScalarGridSpec(
        num_scalar_prefetch=3,
        grid=(nt,),
        in_specs=[
            pl.BlockSpec(memory_space=pl.ANY),
            pl.BlockSpec((tm, 1), lambda i, te, tv, src: (i, 0)),
            pl.BlockSpec((None, None, d, f), lambda i, te, tv, src: (layer, te[i], 0, 0),
                         pipeline_mode=single),
            pl.BlockSpec((None, None, d, f), lambda i, te, tv, src: (layer, te[i], 0, 0),
                         pipeline_mode=single),
            pl.BlockSpec((None, None, f, d), lambda i, te, tv, src: (layer, te[i], 0, 0),
                         pipeline_mode=single),
        ],
        out_specs=pl.BlockSpec((tm, d), lambda i, te, tv, src: (i, 0)),
        scratch_shapes=[pltpu.VMEM((2, tm, d), F32), pltpu.VMEM((d, 2 * f), BF16),
                        pltpu.VMEM((f, d), BF16), pltpu.SemaphoreType.DMA((2,))],
    )
    return pl.pallas_call(
        functools.partial(_moe_kernel, tm, f),
        grid_spec=grid_spec,
        out_shape=jax.ShapeDtypeStruct((p, d), F32),
        compiler_params=_params(("arbitrary",),
                                2 * tm * d * 4 + 3 * d * f * 4 + 3 * d * f * 2 + 2 * tm * d * 4
                                + tm * d * 6 + tm * 2 * f * 8),
        name="moe_experts",
    )(tile_expert, tile_valid, src_rows, h, gate_w, w1, w3, w2)


def _combine_kernel(tm, pos_ref, x_ref, g_ref, y_hbm, o_ref, ybuf, sem):
    i = pl.program_id(0)
    nt = pl.num_programs(0)
    slot = i % 2

    def issue(t, sl):
        def body(r, _):
            for k in range(2):
                row = pos_ref[2 * (t * tm + r) + k]
                pltpu.make_async_copy(y_hbm.at[pl.ds(row, 1)], ybuf.at[sl, k, pl.ds(r, 1)],
                                      sem.at[sl]).start()
            return 0
        lax.fori_loop(0, tm, body, 0, unroll=4)

    @pl.when(i == 0)
    def _():
        issue(0, 0)

    @pl.when(i + 1 < nt)
    def _():
        issue(i + 1, 1 - slot)

    for k in range(2):
        pltpu.make_async_copy(y_hbm.at[pl.ds(0, tm)], ybuf.at[slot, k], sem.at[slot]).wait()
    o_ref[...] = x_ref[...] + g_ref[...] * (ybuf[slot, 0] + ybuf[slot, 1])


def moe_combine(x, gate, y, pos, *, s_len, n_batch, tm=128):
    r, d = x.shape
    cls = _cls_of_tile(tm, s_len, s_len * n_batch, n_batch)
    grid_spec = pltpu.PrefetchScalarGridSpec(
        num_scalar_prefetch=1,
        grid=(r // tm,),
        in_specs=[
            pl.BlockSpec((tm, d), lambda i, pos: (i, 0)),
            pl.BlockSpec((None, 1, d), lambda i, pos: (cls(i), 0, 0)),
            pl.BlockSpec(memory_space=pl.ANY),
        ],
        out_specs=pl.BlockSpec((tm, d), lambda i, pos: (i, 0)),
        scratch_shapes=[pltpu.VMEM((2, 2, tm, d), F32), pltpu.SemaphoreType.DMA((2,))],
    )
    return pl.pallas_call(
        functools.partial(_combine_kernel, tm),
        grid_spec=grid_spec,
        out_shape=jax.ShapeDtypeStruct((r, d), F32),
        compiler_params=_params(("arbitrary",), 4 * tm * d * 4 + 4 * tm * d * 4 + 2 * tm * d * 4),
        name="moe_combine",
    )(pos, x, gate, y)


def _route_plan(rinfo, n_experts, tm):
    r = rinfo.shape[0]
    e = rinfo[:, 0:2].astype(jnp.int32).reshape(-1)
    wgt = rinfo[:, 2:4].reshape(-1)
    n_assign = 2 * r
    onehot = (e[:, None] == jnp.arange(n_experts, dtype=jnp.int32)[None, :]).astype(jnp.int32)
    csum = jnp.cumsum(onehot, axis=0)
    rank = jnp.sum(csum * onehot, axis=1) - 1
    counts = csum[-1]
    ntile = (counts + tm - 1) // tm
    tend = jnp.cumsum(ntile)
    tstart = tend - ntile
    pos = tstart[e] * tm + rank
    n_tiles = n_assign // tm + n_experts
    p = n_tiles * tm
    src = jnp.zeros((p,), jnp.int32).at[pos].set(jnp.arange(n_assign, dtype=jnp.int32) // 2)
    gw = jnp.zeros((p,), F32).at[pos].set(wgt)
    tid = jnp.arange(n_tiles, dtype=jnp.int32)
    tile_valid = (tid < tend[-1]).astype(jnp.int32)
    tile_expert = jnp.minimum(jnp.sum((tid[:, None] >= tend[None, :]).astype(jnp.int32), axis=1),
                              n_experts - 1)
    last = jnp.maximum(tend[-1] - 1, 0)
    tile_expert = jnp.where(tile_valid == 1, tile_expert, tile_expert[last])
    return pos, src, gw.reshape(p, 1), tile_expert, tile_valid


def _final_norm_kernel(x_ref, g_ref, o_ref):
    x = x_ref[...]
    ms = jnp.mean(x * x, axis=-1, keepdims=True)
    o_ref[...] = x * lax.rsqrt(ms + EPS) * g_ref[...]


def final_norm(x, g, n_rows, tm=256):
    d = x.shape[1]
    return pl.pallas_call(
        _final_norm_kernel,
        grid=(n_rows // tm,),
        in_specs=[pl.BlockSpec((tm, d), lambda i: (i, 0)), pl.BlockSpec((1, d), lambda i: (0, 0))],
        out_specs=pl.BlockSpec((tm, d), lambda i: (i, 0)),
        out_shape=jax.ShapeDtypeStruct((n_rows, d), F32),
        compiler_params=_params(("arbitrary",), 4 * tm * d * 4),
        name="final_norm",
    )(x, g.reshape(1, d))


def _rope_tables(s_len, hd, tm):
    rows = s_len // ROPE_GRID_W
    t_row = jnp.repeat(jnp.arange(rows), ROPE_GRID_W).astype(F32)
    t_col = jnp.tile(jnp.arange(ROPE_GRID_W), rows).astype(F32)
    n_f = hd // 4
    inv = ROPE_THETA ** (-jnp.arange(n_f, dtype=F32) / n_f)
    ang = jnp.concatenate([t_row[:, None] * inv, t_col[:, None] * inv], axis=-1)
    cos, sin = jnp.cos(ang), jnp.sin(ang)
    cos2 = jnp.concatenate([cos, cos], axis=-1)
    sin2 = jnp.concatenate([-sin, sin], axis=-1)
    cos2 = jnp.concatenate([cos2, jnp.ones((tm, hd), F32)], axis=0)
    sin2 = jnp.concatenate([sin2, jnp.zeros((tm, hd), F32)], axis=0)
    return cos2, sin2


def kernel(x, c, ctx, c_ctx, ada_w, ada_b, norm_mix, norm_ffn, final_norm_g, lru_w_in, lru_conv_w,
           lru_conv_b, lru_w_a, lru_b_a, lru_w_i, lru_b_i, lru_lam, lru_w_out, attn_w_qkv, attn_q_norm,
           attn_k_norm, attn_w_o, moe_w_rg, moe_b_rg, moe_w_re, moe_b_re, moe_w1, moe_w3, moe_w2):
    n_batch, s_len, d = x.shape
    c_len = ctx.shape[1]
    depth = ada_w.shape[0]
    n_ada = ada_w.shape[2] // d
    d_rnn = lru_w_in.shape[2] // 2
    hd = attn_q_norm.shape[1]
    n_heads = d // hd
    n_kv = (attn_w_qkv.shape[2] // hd - n_heads) // 2
    n_groups = moe_w_rg.shape[2]
    n_experts = moe_w_re.shape[2]
    per_group = n_experts // n_groups
    n_lat = n_batch * s_len
    tm = 512 if (s_len % 512 == 0 and (n_batch * c_len) % 512 == 0) else 256
    moe_tm = 256
    geo = dict(s_len=s_len, n_batch=n_batch)
    assert n_batch + 1 <= NCLS_PAD and n_groups + n_experts <= LANES

    xs = jnp.concatenate([x.reshape(n_lat, d), ctx.reshape(n_batch * c_len, d)], axis=0)

    cond = jnp.zeros((NCLS_PAD, d), F32).at[:n_batch].set(c).at[n_batch].set(c_ctx)
    mods = ada_all(cond, ada_w, ada_b).reshape(depth, NCLS_PAD, n_ada, d)
    cos_tbl, sin_tbl = _rope_tables(s_len, hd, tm)

    def cls_vec(v):
        return v.reshape(NCLS_PAD, 1, d)

    n_mixers = 2
    for l in range(depth):
        sh_m, sc_m, g_m, sh_f, sc_f, g_f = [mods[l, :, j] for j in range(n_ada)]
        j = l // n_mixers
        h = modulate(xs, cls_vec(norm_mix[l] * (1.0 + sc_m)), cls_vec(sh_m), **geo)
        if l % n_mixers == 0:
            yg = matmul(h, lru_w_in, j, n_cols=d_rnn, col0=0, out_dtype=BF16, act="gelu", tm=tm)
            xr = matmul(h, lru_w_in, j, n_cols=d_rnn, col0=d_rnn, out_dtype=F32, tm=tm)
            z_lat, z_ctx = lru_mix(xr, yg, j, lru_conv_w, lru_conv_b, lru_w_a, lru_w_i, lru_b_a,
                                   lru_b_i, lru_lam, c_len=c_len, **geo)
            xs = matmul_resid(z_lat, lru_w_out, j, xs, cls_vec(g_m), a_tail=z_ctx, tm=tm, **geo)
        else:
            q_scale = hd ** -0.5 * math.log2(math.e)
            gain = jnp.concatenate([jnp.tile(attn_q_norm[j] * q_scale, n_heads),
                                    jnp.tile(attn_k_norm[j], n_kv),
                                    jnp.ones((n_kv * hd,), F32)]).reshape(1, -1)
            qkv = matmul_qkv(h, attn_w_qkv, j, gain, cos_tbl, sin_tbl,
                             n_norm_cols=(n_heads + n_kv) * hd, hd=hd, tm=tm, **geo)
            o_lat, o_ctx = attention(qkv, n_heads=n_heads, n_kv=n_kv, hd=hd, c_len=c_len, d_out=d, **geo)
            xs = matmul_resid(o_lat, attn_w_o, j, xs, cls_vec(g_m), a_tail=o_ctx, tm=tm, **geo)

        wr = jnp.zeros((d, LANES), F32).at[:, :n_groups].set(moe_w_rg[l])
        wr = wr.at[:, n_groups:n_groups + n_experts].set(moe_w_re[l])
        br = jnp.zeros((1, LANES), F32).at[0, :n_groups].set(moe_b_rg[l])
        br = br.at[0, n_groups:n_groups + n_experts].set(moe_b_re[l])
        hf, rinfo = modulate_router(xs, cls_vec(norm_ffn[l] * (1.0 + sc_f)), cls_vec(sh_f), wr, br,
                                    n_groups=n_groups, per_group=per_group, **geo)
        pos, src, gw, tile_expert, tile_valid = _route_plan(rinfo, n_experts, moe_tm)
        y = moe_experts(hf, tile_expert, tile_valid, src, gw, moe_w1, moe_w3, moe_w2, l, tm=moe_tm)
        xs = moe_combine(xs, cls_vec(g_f), y, pos, **geo)

    out = final_norm(xs, final_norm_g, n_lat)
    return out.reshape(n_batch, s_len, d)
```

```python
import functools
import math

import jax
import jax.numpy as jnp
from jax import lax
from jax.experimental import pallas as pl
from jax.experimental.pallas import tpu as pltpu

EPS = 1e-6
LRU_C = 8.0
ROPE_THETA = 10000.0
ROPE_GRID_W = 64
LANES = 128
SUBLANES = 8
NCLS_PAD = 8
V7X_VMEM_CAP = 56 * 1024 * 1024
BF16 = jnp.bfloat16
F32 = jnp.float32


def _vmem_limit(nbytes):
    return int(min(V7X_VMEM_CAP, max(16 * 1024 * 1024, nbytes * 5 // 4 + (4 << 20))))


def _params(sem, nbytes):
    return pltpu.CompilerParams(dimension_semantics=sem, vmem_limit_bytes=_vmem_limit(nbytes))


def _sigmoid(x):
    return 1.0 / (1.0 + jnp.exp(-x))


def _gelu_tanh(x):
    return 0.5 * x * (1.0 + jnp.tanh(math.sqrt(2.0 / math.pi) * (x + 0.044715 * (x * x * x))))


def _ada_kernel(c_ref, w_ref, b_ref, o_ref):
    c = c_ref[...]
    s = (c * _sigmoid(c)).astype(BF16)
    w = w_ref[...].astype(BF16)
    o_ref[...] = jnp.dot(s, w, preferred_element_type=F32) + b_ref[...]


def ada_all(cond, ada_w, ada_b):
    depth, d, n = ada_w.shape
    tn = 512
    return pl.pallas_call(
        _ada_kernel,
        grid=(depth, n // tn),
        in_specs=[
            pl.BlockSpec((NCLS_PAD, d), lambda l, j: (0, 0)),
            pl.BlockSpec((None, d, tn), lambda l, j: (l, 0, j)),
            pl.BlockSpec((None, 1, tn), lambda l, j: (l, 0, j)),
        ],
        out_specs=pl.BlockSpec((None, NCLS_PAD, tn), lambda l, j: (l, 0, j)),
        out_shape=jax.ShapeDtypeStruct((depth, NCLS_PAD, n), F32),
        compiler_params=_params(("arbitrary", "arbitrary"), 2 * d * tn * 4 + d * tn * 2),
        name="ada_all",
    )(cond, ada_w, ada_b.reshape(depth, 1, n))


def _modulate(x, gs, sh):
    ms = jnp.mean(x * x, axis=-1, keepdims=True)
    return x * lax.rsqrt(ms + EPS) * gs + sh


def _mod_kernel(x_ref, gs_ref, sh_ref, o_ref):
    o_ref[...] = _modulate(x_ref[...], gs_ref[...], sh_ref[...]).astype(o_ref.dtype)


def _pack_bf16_pairs(h):
    half = h.shape[1] // 2
    bits = lax.bitcast_convert_type(h.astype(BF16).astype(F32), jnp.uint32)
    return (bits[:, :half] >> 16) | (bits[:, half:] & jnp.uint32(0xFFFF0000))


def _unpack_bf16_pairs(words):
    lo = lax.bitcast_convert_type(words << 16, F32).astype(BF16)
    hi = lax.bitcast_convert_type(words & jnp.uint32(0xFFFF0000), F32).astype(BF16)
    return jnp.concatenate([lo, hi], axis=1)


def _mod_router_kernel(n_groups, per_group, x_ref, gs_ref, sh_ref, wr_ref, br_ref, h_ref, r_ref):
    h = _modulate(x_ref[...], gs_ref[...], sh_ref[...])
    h_ref[...] = _pack_bf16_pairs(h)
    logits = jnp.dot(h, wr_ref[...], preferred_element_type=F32,
                     precision=lax.Precision.HIGHEST) + br_ref[...]
    lane = lax.broadcasted_iota(jnp.int32, logits.shape, 1)
    neg = jnp.float32(-jnp.inf)
    big = jnp.int32(1 << 20)
    gl = jnp.where(lane < n_groups, logits, neg)
    gmax = jnp.max(gl, axis=-1, keepdims=True)
    gsum = jnp.sum(jnp.exp(gl - gmax), axis=-1, keepdims=True)
    g_top = 1.0 / gsum
    g_idx = jnp.min(jnp.where(gl == gmax, lane, big), axis=-1, keepdims=True)
    lo = n_groups + per_group * g_idx
    el = jnp.where((lane >= lo) & (lane < lo + per_group), logits, neg)
    m1 = jnp.max(el, axis=-1, keepdims=True)
    i1 = jnp.min(jnp.where(el == m1, lane, big), axis=-1, keepdims=True)
    el2 = jnp.where(lane == i1, neg, el)
    m2 = jnp.max(el2, axis=-1, keepdims=True)
    i2 = jnp.min(jnp.where(el2 == m2, lane, big), axis=-1, keepdims=True)
    t = jnp.exp(m2 - m1)
    w1 = g_top / (1.0 + t)
    w2 = g_top * t / (1.0 + t)
    e1 = (i1 - n_groups).astype(F32)
    e2 = (i2 - n_groups).astype(F32)
    r_ref[...] = jnp.where(lane == 0, e1, jnp.where(lane == 1, e2,
                           jnp.where(lane == 2, w1, jnp.where(lane == 3, w2, 0.0))))


def _cls_of_tile(tm, s_len, n_lat, n_batch):
    def f(i):
        r0 = i * tm
        return jnp.where(r0 < n_lat, r0 // s_len, n_batch)
    return f


def modulate(x, gs, sh, *, s_len, n_batch, tm=256):
    r, d = x.shape
    cls = _cls_of_tile(tm, s_len, s_len * n_batch, n_batch)
    return pl.pallas_call(
        _mod_kernel,
        grid=(r // tm,),
        in_specs=[
            pl.BlockSpec((tm, d), lambda i: (i, 0)),
            pl.BlockSpec((None, 1, d), lambda i: (cls(i), 0, 0)),
            pl.BlockSpec((None, 1, d), lambda i: (cls(i), 0, 0)),
        ],
        out_specs=pl.BlockSpec((tm, d), lambda i: (i, 0)),
        out_shape=jax.ShapeDtypeStruct((r, d), BF16),
        compiler_params=_params(("arbitrary",), 2 * tm * d * 6),
        name="modulate",
    )(x, gs, sh)


def modulate_router(x, gs, sh, wr, br, *, n_rows, n_groups, per_group, s_len, n_batch, tm=256):
    r, d = n_rows, x.shape[1]
    assert r % tm == 0 and r <= x.shape[0]
    cls = _cls_of_tile(tm, s_len, s_len * n_batch, n_batch)
    return pl.pallas_call(
        functools.partial(_mod_router_kernel, n_groups, per_group),
        grid=(r // tm,),
        in_specs=[
            pl.BlockSpec((tm, d), lambda i: (i, 0)),
            pl.BlockSpec((None, 1, d), lambda i: (cls(i), 0, 0)),
            pl.BlockSpec((None, 1, d), lambda i: (cls(i), 0, 0)),
            pl.BlockSpec((d, LANES), lambda i: (0, 0)),
            pl.BlockSpec((1, LANES), lambda i: (0, 0)),
        ],
        out_specs=[
            pl.BlockSpec((tm, d // 2), lambda i: (i, 0)),
            pl.BlockSpec((tm, LANES), lambda i: (i, 0)),
        ],
        out_shape=[jax.ShapeDtypeStruct((r, d // 2), jnp.uint32),
                   jax.ShapeDtypeStruct((r, LANES), F32)],
        compiler_params=_params(("arbitrary",), 2 * tm * d * 8 + 2 * d * LANES * 4 + 4 * tm * d * 4),
        name="modulate_router",
    )(x, gs, sh, wr, br)


def _cast_weight_once(w_ref, wbf_ref):
    @pl.when(pl.program_id(1) == 0)
    def _():
        wbf_ref[...] = w_ref[...].astype(BF16)


def _mm_plain_kernel(act, a_ref, w_ref, o_ref, wbf_ref):
    _cast_weight_once(w_ref, wbf_ref)
    acc = jnp.dot(a_ref[...], wbf_ref[...], preferred_element_type=F32)
    if act == "gelu":
        acc = _gelu_tanh(acc)
    o_ref[...] = acc.astype(o_ref.dtype)


def _mm_resid_kernel(n_main_tiles, a_ref, a2_ref, w_ref, x_ref, g_ref, o_ref, wbf_ref):
    _cast_weight_once(w_ref, wbf_ref)
    i = pl.program_id(1)

    def emit(src_ref):
        acc = jnp.dot(src_ref[...], wbf_ref[...], preferred_element_type=F32)
        o_ref[...] = x_ref[...] + g_ref[...] * acc

    @pl.when(i < n_main_tiles)
    def _():
        emit(a_ref)

    @pl.when(i >= n_main_tiles)
    def _():
        emit(a2_ref)


def _mm_qkv_kernel(n_norm_tiles, hd, a_ref, w_ref, gain_ref, cos_ref, sin_ref, o_ref, wbf_ref):
    _cast_weight_once(w_ref, wbf_ref)
    acc = jnp.dot(a_ref[...], wbf_ref[...], preferred_element_type=F32)
    j = pl.program_id(0)

    @pl.when(j < n_norm_tiles)
    def _():
        cos = cos_ref[...]
        sin = sin_ref[...]
        for h in range(acc.shape[1] // hd):
            xh = acc[:, h * hd:(h + 1) * hd]
            ms = jnp.mean(xh * xh, axis=-1, keepdims=True)
            y = xh * lax.rsqrt(ms + EPS) * gain_ref[:, h * hd:(h + 1) * hd]
            y = y * cos + pltpu.roll(y, hd // 2, 1) * sin
            o_ref[:, h * hd:(h + 1) * hd] = y.astype(o_ref.dtype)

    @pl.when(j >= n_norm_tiles)
    def _():
        o_ref[...] = acc.astype(o_ref.dtype)


def _w_spec(k, tn, layer, col0):
    assert col0 % tn == 0
    index_map = lambda j, i: (layer, 0, col0 // tn + j)
    if _w_buffers(k, tn) == 1:
        return pl.BlockSpec((None, k, tn), index_map, pipeline_mode=pl.Buffered(1))
    return pl.BlockSpec((None, k, tn), index_map)


def _w_buffers(k, tn):
    return 1 if k * tn * 4 > 8 * 1024 * 1024 else 2


def _mm_vmem(tm, k, tn, out_bytes, extra=0):
    return (2 * (tm * k * 2 + tm * tn * out_bytes) + _w_buffers(k, tn) * k * tn * 4 + k * tn * 2
            + 2 * tm * tn * 4 + extra)


def matmul(a, w, layer, *, n_cols, col0=0, out_dtype, act=None, tm=512, tn=1024):
    r, k = a.shape
    tn = min(tn, n_cols)
    assert r % tm == 0 and n_cols % tn == 0
    ob = jnp.dtype(out_dtype).itemsize
    return pl.pallas_call(
        functools.partial(_mm_plain_kernel, act),
        grid=(n_cols // tn, r // tm),
        in_specs=[pl.BlockSpec((tm, k), lambda j, i: (i, 0)), _w_spec(k, tn, layer, col0)],
        out_specs=pl.BlockSpec((tm, tn), lambda j, i: (i, j)),
        out_shape=jax.ShapeDtypeStruct((r, n_cols), out_dtype),
        scratch_shapes=[pltpu.VMEM((k, tn), BF16)],
        compiler_params=_params(("arbitrary", "arbitrary"), _mm_vmem(tm, k, tn, ob)),
        name="matmul_" + (act or "plain"),
    )(a, w)


def matmul_resid(a, w, layer, xres, gate, *, s_len, n_batch, a_tail=None, tm=512, tn=512):
    n = xres.shape[1]
    k = a.shape[1]
    tn = min(tn, n)
    n_main = a.shape[0] // tm
    if a_tail is None:
        a_tail, n_tail, r = a, 1, a.shape[0]
    else:
        n_tail, r = a_tail.shape[0] // tm, a.shape[0] + a_tail.shape[0]
    assert r % tm == 0 and n % tn == 0 and a.shape[0] % tm == 0 and r <= xres.shape[0]
    cls = _cls_of_tile(tm, s_len, s_len * n_batch, n_batch)
    return pl.pallas_call(
        functools.partial(_mm_resid_kernel, n_main),
        grid=(n // tn, r // tm),
        in_specs=[pl.BlockSpec((tm, k), lambda j, i: (jnp.minimum(i, n_main - 1), 0)),
                  pl.BlockSpec((tm, k), lambda j, i: (jnp.clip(i - n_main, 0, n_tail - 1), 0)),
                  _w_spec(k, tn, layer, 0),
                  pl.BlockSpec((tm, tn), lambda j, i: (i, j)),
                  pl.BlockSpec((None, 1, tn), lambda j, i: (cls(i), 0, j))],
        out_specs=pl.BlockSpec((tm, tn), lambda j, i: (i, j)),
        out_shape=jax.ShapeDtypeStruct((r, n), F32),
        scratch_shapes=[pltpu.VMEM((k, tn), BF16)],
        compiler_params=_params(("arbitrary", "arbitrary"),
                                _mm_vmem(tm, k, tn, 4, 2 * tm * k * 2 + 2 * tm * tn * 4)),
        name="matmul_resid",
    )(a, a_tail, w, xres, gate)


def matmul_qkv(a, w, layer, gain, cos_tbl, sin_tbl, *, n_norm_cols, hd, s_len, n_batch, tm=512, tn=1024):
    r, k = a.shape
    n = w.shape[2]
    while n_norm_cols % tn or n % tn:
        tn //= 2
    assert tn % hd == 0 and s_len % tm == 0
    n_lat = s_len * n_batch
    pos_blocks = s_len // tm

    def pos(i):
        return jnp.where(i * tm < n_lat, i % pos_blocks, pos_blocks)

    return pl.pallas_call(
        functools.partial(_mm_qkv_kernel, n_norm_cols // tn, hd),
        grid=(n // tn, r // tm),
        in_specs=[pl.BlockSpec((tm, k), lambda j, i: (i, 0)),
                  _w_spec(k, tn, layer, 0),
                  pl.BlockSpec((1, tn), lambda j, i: (0, j)),
                  pl.BlockSpec((tm, hd), lambda j, i: (pos(i), 0)),
                  pl.BlockSpec((tm, hd), lambda j, i: (pos(i), 0))],
        out_specs=pl.BlockSpec((tm, tn), lambda j, i: (i, j)),
        out_shape=jax.ShapeDtypeStruct((r, n), BF16),
        scratch_shapes=[pltpu.VMEM((k, tn), BF16)],
        compiler_params=_params(("arbitrary", "arbitrary"),
                                _mm_vmem(tm, k, tn, 2, 4 * tm * hd * 4 + tm * tn * 4)),
        name="matmul_qkv",
    )(a, w, gain, cos_tbl, sin_tbl)


def _seg_geometry(t_len):
    ls = t_len // SUBLANES
    assert ls % SUBLANES == 0
    return ls, ls + SUBLANES


def _lru_kernel(tc, xl_ref, xc_ref, yl_ref, yc_ref, cw_ref, cb_ref, wa_ref, wi_ref, ba_ref, bi_ref,
                lam_ref, ol_ref, oc_ref, cvl_ref, cvc_ref, accl_ref, accc_ref,
                al_ref, ul_ref, ac_ref, uc_ref):
    w = xl_ref.shape[1]
    cw = cw_ref[...]
    cb = cb_ref[...]

    def conv_seq(x_ref, cv_ref):
        t_len = x_ref.shape[0]
        nchunk = t_len // tc

        def body(ci, _):
            t0 = pl.multiple_of(ci * tc, tc)
            main = x_ref[pl.ds(t0, tc), :]
            p0 = pl.multiple_of(jnp.maximum(t0 - 8, 0), 8)
            n0 = pl.multiple_of(jnp.minimum(t0 + tc, t_len - 8), 8)
            prev8 = jnp.where(ci > 0, x_ref[pl.ds(p0, 8), :], 0.0)
            next8 = jnp.where(ci < nchunk - 1, x_ref[pl.ds(n0, 8), :], 0.0)
            ext = jnp.concatenate([prev8, main, next8], axis=0)
            ne = tc + 16
            xm1 = pltpu.roll(ext, 1, 0)[8:8 + tc]
            xp1 = pltpu.roll(ext, ne - 1, 0)[8:8 + tc]
            xp2 = pltpu.roll(ext, ne - 2, 0)[8:8 + tc]
            cv_ref[pl.ds(t0, tc), :] = (cw[0:1] * xm1 + cw[1:2] * main + cw[2:3] * xp1
                                        + cw[3:4] * xp2 + cb)
            return 0

        lax.fori_loop(0, nchunk, body, 0)

    conv_seq(xl_ref, cvl_ref)
    conv_seq(xc_ref, cvc_ref)

    def run_seq(d, cv_ref, acc_ref, a_ref, u_ref, carry_in):
        reverse = d == 1
        t_len = cv_ref.shape[0]
        ls, stride = _seg_geometry(t_len)
        piece = min(tc, ls)
        n_lt = w // LANES
        wa, wi, ba, bi, lam = wa_ref[d], wi_ref[d], ba_ref[d], bi_ref[d], lam_ref[d]
        nsp = -LRU_C * (jnp.maximum(-lam, 0.0) + jnp.log(1.0 + jnp.exp(-jnp.abs(lam))))
        c0 = (0.5 * math.log2(math.e)) * nsp

        def seg_row(t):
            return pl.multiple_of(t + SUBLANES * (t // ls), SUBLANES)

        def coeff_body(ci, _):
            t0 = pl.multiple_of(ci * tc, tc)
            x = cv_ref[pl.ds(t0, tc), :]
            xb = x.astype(BF16)
            tr = jnp.tanh(jnp.dot(xb, wa, preferred_element_type=F32) + ba)
            ti = jnp.tanh(jnp.dot(xb, wi, preferred_element_type=F32) + bi)
            a = jnp.exp2(c0 + c0 * tr)
            u = jnp.sqrt(1.0 - a * a) * ((0.5 + 0.5 * ti) * x)
            for p in range(tc // piece):
                row = seg_row(t0 + p * piece)
                for lt in range(n_lt):
                    lanes = slice(lt * LANES, (lt + 1) * LANES)
                    a_ref[lt, pl.ds(row, piece), :] = a[p * piece:(p + 1) * piece, lanes]
                    u_ref[lt, pl.ds(row, piece), :] = u[p * piece:(p + 1) * piece, lanes]
            return 0

        lax.fori_loop(0, t_len // tc, coeff_body, 0, unroll=2 if (t_len // tc) % 2 == 0 else 1)

        def scan_body(k, carry):
            r = (ls - 1 - k) if reverse else k
            idx = pl.ds(r, SUBLANES, stride=stride)
            out = []
            for lt in range(n_lt):
                h, p = carry[lt]
                a = a_ref[lt, idx, :]
                h = a * h + u_ref[lt, idx, :]
                p = a * p
                u_ref[lt, idx, :] = h
                a_ref[lt, idx, :] = p
                out.append((h, p))
            return tuple(out)

        init = tuple((jnp.zeros((SUBLANES, LANES), F32), jnp.ones((SUBLANES, LANES), F32))
                     for _ in range(n_lt))
        ends = lax.fori_loop(0, ls, scan_body, init, unroll=8)
        h_end = jnp.concatenate([e[0] for e in ends], axis=1)
        p_end = jnp.concatenate([e[1] for e in ends], axis=1)

        c = carry_in
        seg_in = [None] * SUBLANES
        for s in (range(SUBLANES - 1, -1, -1) if reverse else range(SUBLANES)):
            seg_in[s] = c
            c = p_end[s:s + 1] * c + h_end[s:s + 1]
        for s in range(SUBLANES):
            for q in range(ls // piece):
                src = s * stride + q * piece
                dst = s * ls + q * piece
                for lt in range(n_lt):
                    lanes = slice(lt * LANES, (lt + 1) * LANES)
                    h = (u_ref[lt, src:src + piece, :]
                         + a_ref[lt, src:src + piece, :] * seg_in[s][:, lanes])
                    if d == 0:
                        acc_ref[dst:dst + piece, lanes] = h
                    else:
                        acc_ref[dst:dst + piece, lanes] = acc_ref[dst:dst + piece, lanes] + h
        return c

    for d in range(2):
        carry = run_seq(d, cvc_ref, accc_ref, ac_ref, uc_ref, jnp.zeros((1, w), F32))
        run_seq(d, cvl_ref, accl_ref, al_ref, ul_ref, carry)

    def finish(acc_ref, y_ref, o_ref):
        nchunk = acc_ref.shape[0] // tc

        def body(ci, _):
            t0 = pl.multiple_of(ci * tc, tc)
            o_ref[pl.ds(t0, tc), :] = (acc_ref[pl.ds(t0, tc), :]
                                       * y_ref[pl.ds(t0, tc), :].astype(F32)).astype(o_ref.dtype)
            return 0

        lax.fori_loop(0, nchunk, body, 0)

    finish(accl_ref, yl_ref, ol_ref)
    finish(accc_ref, yc_ref, oc_ref)


def lru_mix(xr, yg, layer, conv_w, conv_b, w_a, w_i, b_a, b_i, lam, *, s_len, c_len, n_batch, tc=256):
    r, rn = xr.shape
    nblk, bw = w_a.shape[2], w_a.shape[3]
    assert s_len % tc == 0 and c_len % tc == 0 and (n_batch * s_len) % c_len == 0
    ctx0 = n_batch * s_len // c_len
    lat = lambda b, k: (b, k)
    ctx = lambda b, k: (ctx0 + b, k)
    vec = lambda b, k: (layer, 0, 0, k)
    seg_rows = lambda t: SUBLANES * _seg_geometry(t)[1]
    seg_buf = lambda t: pltpu.VMEM((bw // LANES, seg_rows(t), LANES), F32)
    n_dir = w_a.shape[1]
    return pl.pallas_call(
        functools.partial(_lru_kernel, tc),
        grid=(n_batch, nblk),
        in_specs=[
            pl.BlockSpec((s_len, bw), lat), pl.BlockSpec((c_len, bw), ctx),
            pl.BlockSpec((s_len, bw), lat), pl.BlockSpec((c_len, bw), ctx),
            pl.BlockSpec((None, conv_w.shape[1], bw), lambda b, k: (layer, 0, k)),
            pl.BlockSpec((None, 1, bw), lambda b, k: (layer, 0, k)),
            pl.BlockSpec((None, n_dir, None, bw, bw), lambda b, k: (layer, 0, k, 0, 0)),
            pl.BlockSpec((None, n_dir, None, bw, bw), lambda b, k: (layer, 0, k, 0, 0)),
            pl.BlockSpec((None, n_dir, 1, bw), vec), pl.BlockSpec((None, n_dir, 1, bw), vec),
            pl.BlockSpec((None, n_dir, 1, bw), vec),
        ],
        out_specs=[pl.BlockSpec((s_len, bw), lambda b, k: (b, k)),
                   pl.BlockSpec((c_len, bw), lambda b, k: (b, k))],
        out_shape=[jax.ShapeDtypeStruct((n_batch * s_len, rn), BF16),
                   jax.ShapeDtypeStruct((n_batch * c_len, rn), BF16)],
        scratch_shapes=[pltpu.VMEM((s_len, bw), F32), pltpu.VMEM((c_len, bw), F32),
                        pltpu.VMEM((s_len, bw), F32), pltpu.VMEM((c_len, bw), F32),
                        seg_buf(s_len), seg_buf(s_len), seg_buf(c_len), seg_buf(c_len)],
        compiler_params=_params(("arbitrary", "arbitrary"),
                                (s_len + c_len) * bw * (2 * 4 + 2 * 2 + 2 * 2 + 8)
                                + 2 * (seg_rows(s_len) + seg_rows(c_len)) * bw * 4
                                + 8 * bw * bw + 24 * tc * bw * 4),
        name="lru_mix",
    )(xr, xr, yg, yg, conv_w, conv_b.reshape(conv_b.shape[0], 1, rn),
      (0.5 * w_a).astype(BF16), (0.5 * w_i).astype(BF16),
      (0.5 * b_a).reshape(-1, n_dir, 1, rn), (0.5 * b_i).reshape(-1, n_dir, 1, rn),
      lam.reshape(-1, n_dir, 1, rn))


_DN_NT = (((1,), (1,)), ((), ()))


def _split_heads(q, groups, hd):
    return jnp.concatenate([q[:, g * hd:(g + 1) * hd] for g in range(groups)], axis=0)


def _attn_scores(q4, k_refs, s_ref, kc):
    off = 0
    for k_ref in k_refs:
        for c0 in range(0, k_ref.shape[0], kc):
            w = min(kc, k_ref.shape[0] - c0)
            s_ref[:, off:off + w] = lax.dot_general(q4, k_ref[c0:c0 + w, :], _DN_NT,
                                                    preferred_element_type=F32)
            off += w


def _attn_softmax(s_ref, p_ref, rb):
    m_rows, n_keys = s_ref.shape
    tiles = [slice(t * LANES, (t + 1) * LANES) for t in range(n_keys // LANES)]
    inv = []
    for r0 in range(0, m_rows, rb):
        rows = slice(r0, r0 + rb)
        mx = s_ref[rows, tiles[0]]
        for t in tiles[1:]:
            mx = jnp.maximum(mx, s_ref[rows, t])
        mrow = jnp.broadcast_to(jnp.max(mx, axis=-1, keepdims=True), (rb, LANES))
        lsum = jnp.zeros((rb, LANES), F32)
        for t in tiles:
            p = jnp.exp2(s_ref[rows, t] - mrow)
            lsum = lsum + p
            p_ref[rows, t] = p.astype(p_ref.dtype)
        inv.append(1.0 / jnp.sum(lsum, axis=-1, keepdims=True))
    return jnp.concatenate(inv, axis=0)


def _attn_pv(p_ref, v_refs):
    off = 0
    o = None
    for v_ref in v_refs:
        n = v_ref.shape[0]
        part = jnp.dot(p_ref[:, off:off + n], v_ref[...], preferred_element_type=F32)
        o = part if o is None else o + part
        off += n
    return o


def _emit_heads(o, o_ref, row0, tq, groups, hd):
    for g in range(groups):
        o_ref[row0:row0 + tq, g * hd:(g + 1) * hd] = o[g * tq:(g + 1) * tq].astype(o_ref.dtype)


def _attn_lat_kernel(tq, groups, hd, kc, qc_ref, qn_ref, kl_ref, vl_ref, kcx_ref, vcx_ref, o_ref,
                     s0_ref, s1_ref, p0_ref, p1_ref):
    keys = [kcx_ref, kl_ref]
    vals = [vcx_ref, vl_ref]

    @pl.when(pl.program_id(2) == 0)
    def _():
        _attn_scores(_split_heads(qc_ref[0:tq, :], groups, hd), keys, s0_ref, kc)

    _attn_scores(_split_heads(qc_ref[tq:2 * tq, :], groups, hd), keys, s1_ref, kc)
    inv = _attn_softmax(s0_ref, p0_ref, tq)
    _emit_heads(_attn_pv(p0_ref, vals) * inv, o_ref, 0, tq, groups, hd)
    _attn_scores(_split_heads(qn_ref[0:tq, :], groups, hd), keys, s0_ref, kc)
    inv = _attn_softmax(s1_ref, p1_ref, tq)
    _emit_heads(_attn_pv(p1_ref, vals) * inv, o_ref, tq, tq, groups, hd)


def _attn_ctx_kernel(tq, groups, hd, kc, q_ref, k_ref, v_ref, o_ref, s_ref, p_ref):
    _attn_scores(_split_heads(q_ref[...], groups, hd), [k_ref], s_ref, kc)
    inv = _attn_softmax(s_ref, p_ref, tq)
    _emit_heads(_attn_pv(p_ref, [v_ref]) * inv, o_ref, 0, tq, groups, hd)


def attention(qkv, *, n_heads, n_kv, hd, s_len, c_len, n_batch, d_out, with_ctx=True, tq=128, kc=512):
    groups = n_heads // n_kv
    gw = groups * hd
    m_rows = groups * tq
    assert s_len % (2 * tq) == 0 and c_len % tq == 0 and (n_batch * s_len) % c_len == 0
    n_pair = s_len // (2 * tq)
    ctx_k0 = n_batch * s_len // c_len
    kv_bytes = 4 * (s_len + c_len) * hd * 2
    o_lat = pl.pallas_call(
        functools.partial(_attn_lat_kernel, tq, groups, hd, kc),
        grid=(n_batch, n_kv, n_pair),
        in_specs=[
            pl.BlockSpec((2 * tq, gw), lambda b, kv, i: (b * n_pair + i, kv)),
            pl.BlockSpec((2 * tq, gw), lambda b, kv, i: (b * n_pair + jnp.minimum(i + 1, n_pair - 1), kv)),
            pl.BlockSpec((s_len, hd), lambda b, kv, i: (b, n_heads + kv)),
            pl.BlockSpec((s_len, hd), lambda b, kv, i: (b, n_heads + n_kv + kv)),
            pl.BlockSpec((c_len, hd), lambda b, kv, i: (ctx_k0 + b, n_heads + kv)),
            pl.BlockSpec((c_len, hd), lambda b, kv, i: (ctx_k0 + b, n_heads + n_kv + kv)),
        ],
        out_specs=pl.BlockSpec((2 * tq, gw), lambda b, kv, i: (b * n_pair + i, kv)),
        out_shape=jax.ShapeDtypeStruct((n_batch * s_len, d_out), BF16),
        scratch_shapes=[pltpu.VMEM((m_rows, s_len + c_len), F32), pltpu.VMEM((m_rows, s_len + c_len), F32),
                        pltpu.VMEM((m_rows, s_len + c_len), BF16),
                        pltpu.VMEM((m_rows, s_len + c_len), BF16)],
        compiler_params=_params(("arbitrary", "arbitrary", "arbitrary"),
                                kv_bytes + 12 * tq * gw * 2 + 2 * m_rows * (s_len + c_len) * 6
                                + 8 * m_rows * kc * 4),
        name="attention_lat",
    )(qkv, qkv, qkv, qkv, qkv, qkv)

    if not with_ctx:
        return o_lat, None
    nq_ctx = c_len // tq
    ctx_q0 = n_batch * s_len // tq
    o_ctx = pl.pallas_call(
        functools.partial(_attn_ctx_kernel, tq, groups, hd, kc),
        grid=(n_batch, n_kv, nq_ctx),
        in_specs=[
            pl.BlockSpec((tq, gw), lambda b, kv, i: (ctx_q0 + b * nq_ctx + i, kv)),
            pl.BlockSpec((c_len, hd), lambda b, kv, i: (ctx_k0 + b, n_heads + kv)),
            pl.BlockSpec((c_len, hd), lambda b, kv, i: (ctx_k0 + b, n_heads + n_kv + kv)),
        ],
        out_specs=pl.BlockSpec((tq, gw), lambda b, kv, i: (b * nq_ctx + i, kv)),
        out_shape=jax.ShapeDtypeStruct((n_batch * c_len, d_out), BF16),
        scratch_shapes=[pltpu.VMEM((m_rows, c_len), F32), pltpu.VMEM((m_rows, c_len), BF16)],
        compiler_params=_params(("arbitrary", "arbitrary", "arbitrary"),
                                4 * c_len * hd * 2 + 4 * tq * gw * 2 + m_rows * c_len * 4
                                + 8 * m_rows * kc * 4),
        name="attention_ctx",
    )(qkv, qkv, qkv)
    return o_lat, o_ctx


def _moe_kernel(layer, tm, f, te_ref, tv_ref, nx_ref, src_ref, h_hbm, w1_hbm, w3_hbm, w2_hbm, y_ref,
                xbuf, st1, st3, st2, w13b, w2b, sem_x, sem_w):
    i = pl.program_id(0)
    nt = pl.num_programs(0)
    slot = i % 2

    def issue_rows(t, sl):
        def body(r, _):
            row = src_ref[t * tm + r]
            pltpu.make_async_copy(h_hbm.at[pl.ds(row, 1)], xbuf.at[sl, pl.ds(r, 1)],
                                  sem_x.at[sl]).start()
            return 0
        lax.fori_loop(0, tm, body, 0, unroll=8)

    def weight_copies(e):
        return (pltpu.make_async_copy(w1_hbm.at[layer, e], st1, sem_w.at[0]),
                pltpu.make_async_copy(w3_hbm.at[layer, e], st3, sem_w.at[1]),
                pltpu.make_async_copy(w2_hbm.at[layer, e], st2, sem_w.at[2]))

    @pl.when(jnp.logical_and(i == 0, tv_ref[0] == 1))
    def _():
        issue_rows(0, 0)
        for cp in weight_copies(te_ref[0]):
            cp.start()

    @pl.when(jnp.logical_and(i + 1 < nt, tv_ref[jnp.minimum(i + 1, nt - 1)] == 1))
    def _():
        issue_rows(i + 1, 1 - slot)

    new_expert = jnp.logical_or(i == 0, te_ref[i] != te_ref[jnp.maximum(i - 1, 0)])

    @pl.when(jnp.logical_and(new_expert, tv_ref[i] == 1))
    def _():
        for cp in weight_copies(te_ref[i]):
            cp.wait()
        w13b[:, :f] = st1[...].astype(BF16)
        w13b[:, f:] = st3[...].astype(BF16)
        w2b[...] = st2[...].astype(BF16)

        @pl.when(nx_ref[i] >= 0)
        def _():
            for cp in weight_copies(nx_ref[i]):
                cp.start()

    @pl.when(tv_ref[i] == 1)
    def _():
        pltpu.make_async_copy(h_hbm.at[pl.ds(0, tm)], xbuf.at[slot], sem_x.at[slot]).wait()
        x = _unpack_bf16_pairs(xbuf[slot])
        h13 = jnp.dot(x, w13b[...], preferred_element_type=F32)
        h1 = h13[:, :f]
        act = (h1 * _sigmoid(h1)) * h13[:, f:]
        y_ref[...] = jnp.dot(act.astype(BF16), w2b[...], preferred_element_type=F32)

    @pl.when(tv_ref[i] == 0)
    def _():
        y_ref[...] = jnp.zeros_like(y_ref)


def moe_experts(hp, tile_expert, tile_valid, next_expert, src_rows, w1, w3, w2, layer, *, tm):
    half = hp.shape[1]
    d = 2 * half
    p = src_rows.shape[0]
    f = w2.shape[2]
    nt = p // tm
    any_spec = pl.BlockSpec(memory_space=pl.ANY)
    grid_spec = pltpu.PrefetchScalarGridSpec(
        num_scalar_prefetch=4,
        grid=(nt,),
        in_specs=[any_spec, any_spec, any_spec, any_spec],
        out_specs=pl.BlockSpec((tm, d), lambda i, te, tv, nx, src: (i, 0)),
        scratch_shapes=[pltpu.VMEM((2, tm, half), jnp.uint32),
                        pltpu.VMEM((d, f), F32), pltpu.VMEM((d, f), F32), pltpu.VMEM((f, d), F32),
                        pltpu.VMEM((d, 2 * f), BF16), pltpu.VMEM((f, d), BF16),
                        pltpu.SemaphoreType.DMA((2,)), pltpu.SemaphoreType.DMA((3,))],
    )
    return pl.pallas_call(
        functools.partial(_moe_kernel, layer, tm, f),
        grid_spec=grid_spec,
        out_shape=jax.ShapeDtypeStruct((p, d), F32),
        compiler_params=_params(("arbitrary",),
                                2 * tm * half * 4 + 3 * d * f * 4 + 3 * d * f * 2 + 2 * tm * d * 4
                                + tm * d * 6 + tm * 2 * f * 8),
        name="moe_experts",
    )(tile_expert, tile_valid, next_expert, src_rows, hp, w1, w3, w2)


def _combine_kernel(tm, mode, pos_ref, x_ref, g_ref, r_ref, y_hbm, *rest):
    if mode == "mid":
        gs_ref, sh_ref, o_ref, h_ref, ybuf, sem = rest
    else:
        gs_ref, o_ref, ybuf, sem = rest
    i = pl.program_id(0)
    nt = pl.num_programs(0)
    slot = i % 2

    def issue(t, sl):
        def body(r, _):
            for k in range(2):
                row = pos_ref[2 * (t * tm + r) + k]
                pltpu.make_async_copy(y_hbm.at[pl.ds(row, 1)], ybuf.at[sl, k, pl.ds(r, 1)],
                                      sem.at[sl]).start()
            return 0
        lax.fori_loop(0, tm, body, 0, unroll=4)

    @pl.when(i == 0)
    def _():
        issue(0, 0)

    @pl.when(i + 1 < nt)
    def _():
        issue(i + 1, 1 - slot)

    for k in range(2):
        pltpu.make_async_copy(y_hbm.at[pl.ds(0, tm)], ybuf.at[slot, k], sem.at[slot]).wait()
    rinfo = r_ref[...]
    moe = rinfo[:, 2:3] * ybuf[slot, 0] + rinfo[:, 3:4] * ybuf[slot, 1]
    x_new = x_ref[...] + g_ref[...] * moe
    if mode == "mid":
        o_ref[...] = x_new
        h_ref[...] = _modulate(x_new, gs_ref[...], sh_ref[...]).astype(h_ref.dtype)
    else:
        ms = jnp.mean(x_new * x_new, axis=-1, keepdims=True)
        o_ref[...] = x_new * lax.rsqrt(ms + EPS) * gs_ref[...]


def moe_combine(x, gate, rinfo, y, pos, gs, sh, *, n_rows, s_len, n_batch, tm=128):
    d = x.shape[1]
    mode = "mid" if sh is not None else "last"
    assert n_rows % tm == 0 and n_rows <= x.shape[0]
    cls = _cls_of_tile(tm, s_len, s_len * n_batch, n_batch)
    row_spec = pl.BlockSpec((tm, d), lambda i, pos: (i, 0))
    cls_spec = pl.BlockSpec((None, 1, d), lambda i, pos: (cls(i), 0, 0))
    in_specs = [row_spec, cls_spec, pl.BlockSpec((tm, LANES), lambda i, pos: (i, 0)),
                pl.BlockSpec(memory_space=pl.ANY)]
    if mode == "mid":
        in_specs += [cls_spec, cls_spec]
        operands = (gs, sh)
        out_specs = [row_spec, row_spec]
        out_shape = [jax.ShapeDtypeStruct((n_rows, d), F32), jax.ShapeDtypeStruct((n_rows, d), BF16)]
    else:
        in_specs += [pl.BlockSpec((1, d), lambda i, pos: (0, 0))]
        operands = (gs.reshape(1, d),)
        out_specs = row_spec
        out_shape = jax.ShapeDtypeStruct((n_rows, d), F32)
    grid_spec = pltpu.PrefetchScalarGridSpec(
        num_scalar_prefetch=1,
        grid=(n_rows // tm,),
        in_specs=in_specs,
        out_specs=out_specs,
        scratch_shapes=[pltpu.VMEM((2, 2, tm, d), F32), pltpu.SemaphoreType.DMA((2,))],
    )
    return pl.pallas_call(
        functools.partial(_combine_kernel, tm, mode),
        grid_spec=grid_spec,
        out_shape=out_shape,
        compiler_params=_params(("arbitrary",), 4 * tm * d * 4 + 6 * tm * d * 4 + 4 * tm * d * 4),
        name="moe_combine_" + mode,
    )(pos, x, gate, rinfo, y, *operands)


def _route_plan(rinfo, n_experts, tm):
    r = rinfo.shape[0]
    e = rinfo[:, 0:2].astype(jnp.int32).reshape(-1)
    n_assign = 2 * r
    onehot = (e[:, None] == jnp.arange(n_experts, dtype=jnp.int32)[None, :]).astype(jnp.int32)
    csum = jnp.cumsum(onehot, axis=0)
    rank = jnp.sum(csum * onehot, axis=1) - 1
    counts = csum[-1]
    ntile = (counts + tm - 1) // tm
    tend = jnp.cumsum(ntile)
    tstart = tend - ntile
    pos = tstart[e] * tm + rank
    n_tiles = n_assign // tm + n_experts
    p = n_tiles * tm
    src = jnp.zeros((p,), jnp.int32).at[pos].set(jnp.arange(n_assign, dtype=jnp.int32) // 2)
    tid = jnp.arange(n_tiles, dtype=jnp.int32)
    tile_valid = (tid < tend[-1]).astype(jnp.int32)
    tile_expert = jnp.minimum(jnp.sum((tid[:, None] >= tend[None, :]).astype(jnp.int32), axis=1),
                              n_experts - 1)
    nxt = tend[tile_expert]
    next_expert = jnp.where(nxt < tend[-1], tile_expert[jnp.minimum(nxt, n_tiles - 1)], -1)
    last = jnp.maximum(tend[-1] - 1, 0)
    tile_expert = jnp.where(tile_valid == 1, tile_expert, tile_expert[last])
    return pos, src, tile_expert, tile_valid, next_expert.astype(jnp.int32)


def _rope_tables(s_len, hd, tm):
    rows = s_len // ROPE_GRID_W
    t_row = jnp.repeat(jnp.arange(rows), ROPE_GRID_W).astype(F32)
    t_col = jnp.tile(jnp.arange(ROPE_GRID_W), rows).astype(F32)
    n_f = hd // 4
    inv = ROPE_THETA ** (-jnp.arange(n_f, dtype=F32) / n_f)
    ang = jnp.concatenate([t_row[:, None] * inv, t_col[:, None] * inv], axis=-1)
    cos, sin = jnp.cos(ang), jnp.sin(ang)
    cos2 = jnp.concatenate([cos, cos], axis=-1)
    sin2 = jnp.concatenate([-sin, sin], axis=-1)
    cos2 = jnp.concatenate([cos2, jnp.ones((tm, hd), F32)], axis=0)
    sin2 = jnp.concatenate([sin2, jnp.zeros((tm, hd), F32)], axis=0)
    return cos2, sin2


def kernel(x, c, ctx, c_ctx, ada_w, ada_b, norm_mix, norm_ffn, final_norm_g, lru_w_in, lru_conv_w,
           lru_conv_b, lru_w_a, lru_b_a, lru_w_i, lru_b_i, lru_lam, lru_w_out, attn_w_qkv, attn_q_norm,
           attn_k_norm, attn_w_o, moe_w_rg, moe_b_rg, moe_w_re, moe_b_re, moe_w1, moe_w3, moe_w2):
    n_batch, s_len, d = x.shape
    c_len = ctx.shape[1]
    depth = ada_w.shape[0]
    n_ada = ada_w.shape[2] // d
    d_rnn = lru_w_in.shape[2] // 2
    hd = attn_q_norm.shape[1]
    n_heads = d // hd
    n_kv = (attn_w_qkv.shape[2] // hd - n_heads) // 2
    n_groups = moe_w_rg.shape[2]
    n_experts = moe_w_re.shape[2]
    per_group = n_experts // n_groups
    n_lat = n_batch * s_len
    tm = 512 if (s_len % 512 == 0 and (n_batch * c_len) % 512 == 0) else 256
    moe_tm = 256
    geo = dict(s_len=s_len, n_batch=n_batch)
    assert n_batch + 1 <= NCLS_PAD and n_groups + n_experts <= LANES

    xs = jnp.concatenate([x.reshape(n_lat, d), ctx.reshape(n_batch * c_len, d)], axis=0)

    cond = jnp.zeros((NCLS_PAD, d), F32).at[:n_batch].set(c).at[n_batch].set(c_ctx)
    mods = ada_all(cond, ada_w, ada_b).reshape(depth, NCLS_PAD, n_ada, d)
    cos_tbl, sin_tbl = _rope_tables(s_len, hd, tm)

    def cls_vec(v):
        return v.reshape(NCLS_PAD, 1, d)

    n_mixers = 2
    n_all = xs.shape[0]

    def mix_mod(l):
        return cls_vec(norm_mix[l] * (1.0 + mods[l, :, 1])), cls_vec(mods[l, :, 0])

    h = modulate(xs, *mix_mod(0), **geo)
    for l in range(depth):
        last = l == depth - 1
        sh_m, sc_m, g_m, sh_f, sc_f, g_f = [mods[l, :, j] for j in range(n_ada)]
        j = l // n_mixers
        if l % n_mixers == 0:
            yg = matmul(h, lru_w_in, j, n_cols=d_rnn, col0=0, out_dtype=BF16, act="gelu", tm=tm)
            xr = matmul(h, lru_w_in, j, n_cols=d_rnn, col0=d_rnn, out_dtype=F32, tm=tm)
            z_lat, z_ctx = lru_mix(xr, yg, j, lru_conv_w, lru_conv_b, lru_w_a, lru_w_i, lru_b_a,
                                   lru_b_i, lru_lam, c_len=c_len, **geo)
            xs = matmul_resid(z_lat, lru_w_out, j, xs, cls_vec(g_m),
                              a_tail=None if last else z_ctx, tm=tm, **geo)
        else:
            q_scale = hd ** -0.5 * math.log2(math.e)
            gain = jnp.concatenate([jnp.tile(attn_q_norm[j] * q_scale, n_heads),
                                    jnp.tile(attn_k_norm[j], n_kv),
                                    jnp.ones((n_kv * hd,), F32)]).reshape(1, -1)
            qkv = matmul_qkv(h, attn_w_qkv, j, gain, cos_tbl, sin_tbl,
                             n_norm_cols=(n_heads + n_kv) * hd, hd=hd, tm=tm, **geo)
            o_lat, o_ctx = attention(qkv, n_heads=n_heads, n_kv=n_kv, hd=hd, c_len=c_len, d_out=d,
                                     with_ctx=not last, **geo)
            xs = matmul_resid(o_lat, attn_w_o, j, xs, cls_vec(g_m), a_tail=o_ctx, tm=tm, **geo)

        n_rows = n_lat if last else n_all
        wr = jnp.zeros((d, LANES), F32).at[:, :n_groups].set(moe_w_rg[l])
        wr = wr.at[:, n_groups:n_groups + n_experts].set(moe_w_re[l])
        br = jnp.zeros((1, LANES), F32).at[0, :n_groups].set(moe_b_rg[l])
        br = br.at[0, n_groups:n_groups + n_experts].set(moe_b_re[l])
        hp, rinfo = modulate_router(xs, cls_vec(norm_ffn[l] * (1.0 + sc_f)), cls_vec(sh_f), wr, br,
                                    n_rows=n_rows, n_groups=n_groups, per_group=per_group, **geo)
        pos, src, tile_expert, tile_valid, next_expert = _route_plan(rinfo, n_experts, moe_tm)
        y = moe_experts(hp, tile_expert, tile_valid, next_expert, src, moe_w1, moe_w3, moe_w2, l,
                        tm=moe_tm)
        if last:
            out = moe_combine(xs, cls_vec(g_f), rinfo, y, pos, final_norm_g, None, n_rows=n_rows, **geo)
        else:
            xs, h = moe_combine(xs, cls_vec(g_f), rinfo, y, pos, *mix_mod(l + 1), n_rows=n_rows, **geo)

    return out.reshape(n_batch, s_len, d)
```

```python
import functools
import math

import jax
import jax.numpy as jnp
from jax import lax
from jax.experimental import pallas as pl
from jax.experimental.pallas import tpu as pltpu

EPS = 1e-6
LRU_C = 8.0
ROPE_THETA = 10000.0
ROPE_GRID_W = 64
LANES = 128
SUBLANES = 8
NCLS_PAD = 8
V7X_VMEM_CAP = 56 * 1024 * 1024
BF16 = jnp.bfloat16
F32 = jnp.float32


def _vmem_limit(nbytes):
    return int(min(V7X_VMEM_CAP, max(16 * 1024 * 1024, nbytes * 5 // 4 + (4 << 20))))


def _params(sem, nbytes):
    return pltpu.CompilerParams(dimension_semantics=sem, vmem_limit_bytes=_vmem_limit(nbytes))


def _sigmoid(x):
    return 1.0 / (1.0 + jnp.exp(-x))


def _gelu_tanh(x):
    return 0.5 * x * (1.0 + jnp.tanh(math.sqrt(2.0 / math.pi) * (x + 0.044715 * (x * x * x))))


def _ada_kernel(c_ref, w_ref, b_ref, o_ref):
    c = c_ref[...]
    s = (c * _sigmoid(c)).astype(BF16)
    w = w_ref[...].astype(BF16)
    o_ref[...] = jnp.dot(s, w, preferred_element_type=F32) + b_ref[...]


def ada_all(cond, ada_w, ada_b):
    depth, d, n = ada_w.shape
    tn = 512
    return pl.pallas_call(
        _ada_kernel,
        grid=(depth, n // tn),
        in_specs=[
            pl.BlockSpec((NCLS_PAD, d), lambda l, j: (0, 0)),
            pl.BlockSpec((None, d, tn), lambda l, j: (l, 0, j)),
            pl.BlockSpec((None, 1, tn), lambda l, j: (l, 0, j)),
        ],
        out_specs=pl.BlockSpec((None, NCLS_PAD, tn), lambda l, j: (l, 0, j)),
        out_shape=jax.ShapeDtypeStruct((depth, NCLS_PAD, n), F32),
        compiler_params=_params(("arbitrary", "arbitrary"), 2 * d * tn * 4 + d * tn * 2),
        name="ada_all",
    )(cond, ada_w, ada_b.reshape(depth, 1, n))


def _modulate(x, gs, sh):
    ms = jnp.mean(x * x, axis=-1, keepdims=True)
    return x * lax.rsqrt(ms + EPS) * gs + sh


def _mod_kernel(x_ref, gs_ref, sh_ref, o_ref):
    o_ref[...] = _modulate(x_ref[...], gs_ref[...], sh_ref[...]).astype(o_ref.dtype)


def _pack_bf16_pairs(h):
    half = h.shape[1] // 2
    bits = lax.bitcast_convert_type(h.astype(BF16).astype(F32), jnp.uint32)
    return (bits[:, :half] >> 16) | (bits[:, half:] & jnp.uint32(0xFFFF0000))


def _unpack_bf16_pairs(words, dtype=BF16):
    lo = lax.bitcast_convert_type(words << 16, F32).astype(dtype)
    hi = lax.bitcast_convert_type(words & jnp.uint32(0xFFFF0000), F32).astype(dtype)
    return jnp.concatenate([lo, hi], axis=1)


def _mod_router_kernel(n_groups, per_group, x_ref, gs_ref, sh_ref, wr_ref, br_ref, h_ref, r_ref):
    h = _modulate(x_ref[...], gs_ref[...], sh_ref[...])
    h_ref[...] = _pack_bf16_pairs(h)
    hh = h.astype(BF16)
    hl = (h - hh.astype(F32)).astype(BF16)
    r_hi = jnp.dot(hh, wr_ref[...], preferred_element_type=F32)
    r_lo = jnp.dot(hl, wr_ref[:, :LANES], preferred_element_type=F32)
    logits = r_hi[:, :LANES] + r_hi[:, LANES:] + r_lo + br_ref[...]
    lane = lax.broadcasted_iota(jnp.int32, logits.shape, 1)
    neg = jnp.float32(-jnp.inf)
    big = jnp.int32(1 << 20)
    gl = jnp.where(lane < n_groups, logits, neg)
    gmax = jnp.max(gl, axis=-1, keepdims=True)
    gsum = jnp.sum(jnp.exp(gl - gmax), axis=-1, keepdims=True)
    g_top = 1.0 / gsum
    g_idx = jnp.min(jnp.where(gl == gmax, lane, big), axis=-1, keepdims=True)
    lo = n_groups + per_group * g_idx
    el = jnp.where((lane >= lo) & (lane < lo + per_group), logits, neg)
    m1 = jnp.max(el, axis=-1, keepdims=True)
    i1 = jnp.min(jnp.where(el == m1, lane, big), axis=-1, keepdims=True)
    el2 = jnp.where(lane == i1, neg, el)
    m2 = jnp.max(el2, axis=-1, keepdims=True)
    i2 = jnp.min(jnp.where(el2 == m2, lane, big), axis=-1, keepdims=True)
    t = jnp.exp(m2 - m1)
    w1 = g_top / (1.0 + t)
    w2 = g_top * t / (1.0 + t)
    e1 = (i1 - n_groups).astype(F32)
    e2 = (i2 - n_groups).astype(F32)
    r_ref[...] = jnp.where(lane == 0, e1, jnp.where(lane == 1, e2,
                           jnp.where(lane == 2, w1, jnp.where(lane == 3, w2, 0.0))))


def _cls_of_tile(tm, s_len, n_lat, n_batch):
    def f(i):
        r0 = i * tm
        return jnp.where(r0 < n_lat, r0 // s_len, n_batch)
    return f


def modulate(x, gs, sh, *, s_len, n_batch, tm=256):
    r, d = x.shape
    cls = _cls_of_tile(tm, s_len, s_len * n_batch, n_batch)
    return pl.pallas_call(
        _mod_kernel,
        grid=(r // tm,),
        in_specs=[
            pl.BlockSpec((tm, d), lambda i: (i, 0)),
            pl.BlockSpec((None, 1, d), lambda i: (cls(i), 0, 0)),
            pl.BlockSpec((None, 1, d), lambda i: (cls(i), 0, 0)),
        ],
        out_specs=pl.BlockSpec((tm, d), lambda i: (i, 0)),
        out_shape=jax.ShapeDtypeStruct((r, d), BF16),
        compiler_params=_params(("arbitrary",), 2 * tm * d * 6),
        name="modulate",
    )(x, gs, sh)


def modulate_router(x, gs, sh, wr, br, *, n_rows, n_groups, per_group, s_len, n_batch, tm=256):
    r, d = n_rows, x.shape[1]
    assert r % tm == 0 and r <= x.shape[0]
    cls = _cls_of_tile(tm, s_len, s_len * n_batch, n_batch)
    return pl.pallas_call(
        functools.partial(_mod_router_kernel, n_groups, per_group),
        grid=(r // tm,),
        in_specs=[
            pl.BlockSpec((tm, d), lambda i: (i, 0)),
            pl.BlockSpec((None, 1, d), lambda i: (cls(i), 0, 0)),
            pl.BlockSpec((None, 1, d), lambda i: (cls(i), 0, 0)),
            pl.BlockSpec((d, 2 * LANES), lambda i: (0, 0)),
            pl.BlockSpec((1, LANES), lambda i: (0, 0)),
        ],
        out_specs=[
            pl.BlockSpec((tm, d // 2), lambda i: (i, 0)),
            pl.BlockSpec((tm, LANES), lambda i: (i, 0)),
        ],
        out_shape=[jax.ShapeDtypeStruct((r, d // 2), jnp.uint32),
                   jax.ShapeDtypeStruct((r, LANES), F32)],
        compiler_params=_params(("arbitrary",), 2 * tm * d * 8 + 2 * d * LANES * 4 + 4 * tm * d * 4),
        name="modulate_router",
    )(x, gs, sh, wr, br)


def _cast_weight_once(w_ref, wbf_ref):
    @pl.when(pl.program_id(1) == 0)
    def _():
        wbf_ref[...] = w_ref[...].astype(BF16)


def _mm_plain_kernel(act, a_ref, w_ref, o_ref, wbf_ref):
    _cast_weight_once(w_ref, wbf_ref)
    acc = jnp.dot(a_ref[...], wbf_ref[...], preferred_element_type=F32)
    if act == "gelu":
        acc = _gelu_tanh(acc)
    o_ref[...] = acc.astype(o_ref.dtype)


def _mm_resid_kernel(n_main_tiles, a_ref, a2_ref, w_ref, x_ref, g_ref, o_ref, wbf_ref):
    _cast_weight_once(w_ref, wbf_ref)
    i = pl.program_id(1)

    def emit(src_ref):
        acc = jnp.dot(src_ref[...], wbf_ref[...], preferred_element_type=F32)
        o_ref[...] = x_ref[...] + g_ref[...] * acc

    @pl.when(i < n_main_tiles)
    def _():
        emit(a_ref)

    @pl.when(i >= n_main_tiles)
    def _():
        emit(a2_ref)


def _mm_qkv_kernel(n_norm_tiles, hd, a_ref, w_ref, gain_ref, cos_ref, sin_ref, o_ref, wbf_ref):
    _cast_weight_once(w_ref, wbf_ref)
    acc = jnp.dot(a_ref[...], wbf_ref[...], preferred_element_type=F32)
    j = pl.program_id(0)

    @pl.when(j < n_norm_tiles)
    def _():
        cos = cos_ref[...]
        sin = sin_ref[...]
        for h in range(acc.shape[1] // hd):
            xh = acc[:, h * hd:(h + 1) * hd]
            ms = jnp.mean(xh * xh, axis=-1, keepdims=True)
            y = xh * lax.rsqrt(ms + EPS) * gain_ref[:, h * hd:(h + 1) * hd]
            y = y * cos + pltpu.roll(y, hd // 2, 1) * sin
            o_ref[:, h * hd:(h + 1) * hd] = y.astype(o_ref.dtype)

    @pl.when(j >= n_norm_tiles)
    def _():
        o_ref[...] = acc.astype(o_ref.dtype)


def _w_spec(k, tn, layer, col0):
    assert col0 % tn == 0
    index_map = lambda j, i: (layer, 0, col0 // tn + j)
    if _w_buffers(k, tn) == 1:
        return pl.BlockSpec((None, k, tn), index_map, pipeline_mode=pl.Buffered(1))
    return pl.BlockSpec((None, k, tn), index_map)


def _w_buffers(k, tn):
    return 1 if k * tn * 4 > 8 * 1024 * 1024 else 2


def _mm_vmem(tm, k, tn, out_bytes, extra=0):
    return (2 * (tm * k * 2 + tm * tn * out_bytes) + _w_buffers(k, tn) * k * tn * 4 + k * tn * 2
            + 2 * tm * tn * 4 + extra)


def matmul(a, w, layer, *, n_cols, col0=0, out_dtype, act=None, tm=512, tn=1024):
    r, k = a.shape
    tn = min(tn, n_cols)
    assert r % tm == 0 and n_cols % tn == 0
    ob = jnp.dtype(out_dtype).itemsize
    return pl.pallas_call(
        functools.partial(_mm_plain_kernel, act),
        grid=(n_cols // tn, r // tm),
        in_specs=[pl.BlockSpec((tm, k), lambda j, i: (i, 0)), _w_spec(k, tn, layer, col0)],
        out_specs=pl.BlockSpec((tm, tn), lambda j, i: (i, j)),
        out_shape=jax.ShapeDtypeStruct((r, n_cols), out_dtype),
        scratch_shapes=[pltpu.VMEM((k, tn), BF16)],
        compiler_params=_params(("arbitrary", "arbitrary"), _mm_vmem(tm, k, tn, ob)),
        name="matmul_" + (act or "plain"),
    )(a, w)


def matmul_resid(a, w, layer, xres, gate, *, s_len, n_batch, a_tail=None, tm=512, tn=512):
    n = xres.shape[1]
    k = a.shape[1]
    tn = min(tn, n)
    n_main = a.shape[0] // tm
    if a_tail is None:
        a_tail, n_tail, r = a, 1, a.shape[0]
    else:
        n_tail, r = a_tail.shape[0] // tm, a.shape[0] + a_tail.shape[0]
    assert r % tm == 0 and n % tn == 0 and a.shape[0] % tm == 0 and r <= xres.shape[0]
    cls = _cls_of_tile(tm, s_len, s_len * n_batch, n_batch)
    return pl.pallas_call(
        functools.partial(_mm_resid_kernel, n_main),
        grid=(n // tn, r // tm),
        in_specs=[pl.BlockSpec((tm, k), lambda j, i: (jnp.minimum(i, n_main - 1), 0)),
                  pl.BlockSpec((tm, k), lambda j, i: (jnp.clip(i - n_main, 0, n_tail - 1), 0)),
                  _w_spec(k, tn, layer, 0),
                  pl.BlockSpec((tm, tn), lambda j, i: (i, j)),
                  pl.BlockSpec((None, 1, tn), lambda j, i: (cls(i), 0, j))],
        out_specs=pl.BlockSpec((tm, tn), lambda j, i: (i, j)),
        out_shape=jax.ShapeDtypeStruct((r, n), F32),
        scratch_shapes=[pltpu.VMEM((k, tn), BF16)],
        compiler_params=_params(("arbitrary", "arbitrary"),
                                _mm_vmem(tm, k, tn, 4, 2 * tm * k * 2 + 2 * tm * tn * 4)),
        name="matmul_resid",
    )(a, a_tail, w, xres, gate)


def matmul_qkv(a, w, layer, gain, cos_tbl, sin_tbl, *, n_norm_cols, hd, s_len, n_batch, tm=512, tn=1024):
    r, k = a.shape
    n = w.shape[2]
    while n_norm_cols % tn or n % tn:
        tn //= 2
    assert tn % hd == 0 and s_len % tm == 0
    n_lat = s_len * n_batch
    pos_blocks = s_len // tm

    def pos(i):
        return jnp.where(i * tm < n_lat, i % pos_blocks, pos_blocks)

    return pl.pallas_call(
        functools.partial(_mm_qkv_kernel, n_norm_cols // tn, hd),
        grid=(n // tn, r // tm),
        in_specs=[pl.BlockSpec((tm, k), lambda j, i: (i, 0)),
                  _w_spec(k, tn, layer, 0),
                  pl.BlockSpec((1, tn), lambda j, i: (0, j)),
                  pl.BlockSpec((tm, hd), lambda j, i: (pos(i), 0)),
                  pl.BlockSpec((tm, hd), lambda j, i: (pos(i), 0))],
        out_specs=pl.BlockSpec((tm, tn), lambda j, i: (i, j)),
        out_shape=jax.ShapeDtypeStruct((r, n), BF16),
        scratch_shapes=[pltpu.VMEM((k, tn), BF16)],
        compiler_params=_params(("arbitrary", "arbitrary"),
                                _mm_vmem(tm, k, tn, 2, 4 * tm * hd * 4 + tm * tn * 4)),
        name="matmul_qkv",
    )(a, w, gain, cos_tbl, sin_tbl)


def _seg_geometry(t_len):
    ls = t_len // SUBLANES
    assert ls % SUBLANES == 0
    return ls, ls + SUBLANES


def _lru_kernel(tc, xl_ref, xc_ref, yl_ref, yc_ref, cw_ref, cb_ref, wa_ref, wi_ref, ba_ref, bi_ref,
                lam_ref, ol_ref, oc_ref, cvl_ref, cvc_ref, accl_ref, accc_ref,
                al_ref, ul_ref, ac_ref, uc_ref):
    w = xl_ref.shape[1]
    cw = cw_ref[...]
    cb = cb_ref[...]

    def conv_seq(x_ref, cv_ref):
        t_len = x_ref.shape[0]
        nchunk = t_len // tc

        def body(ci, _):
            t0 = pl.multiple_of(ci * tc, tc)
            main = x_ref[pl.ds(t0, tc), :]
            p0 = pl.multiple_of(jnp.maximum(t0 - 8, 0), 8)
            n0 = pl.multiple_of(jnp.minimum(t0 + tc, t_len - 8), 8)
            prev8 = jnp.where(ci > 0, x_ref[pl.ds(p0, 8), :], 0.0)
            next8 = jnp.where(ci < nchunk - 1, x_ref[pl.ds(n0, 8), :], 0.0)
            ext = jnp.concatenate([prev8, main, next8], axis=0)
            ne = tc + 16
            xm1 = pltpu.roll(ext, 1, 0)[8:8 + tc]
            xp1 = pltpu.roll(ext, ne - 1, 0)[8:8 + tc]
            xp2 = pltpu.roll(ext, ne - 2, 0)[8:8 + tc]
            cv_ref[pl.ds(t0, tc), :] = (cw[0:1] * xm1 + cw[1:2] * main + cw[2:3] * xp1
                                        + cw[3:4] * xp2 + cb)
            return 0

        lax.fori_loop(0, nchunk, body, 0)

    conv_seq(xl_ref, cvl_ref)
    conv_seq(xc_ref, cvc_ref)

    def run_seq(d, cv_ref, acc_ref, a_ref, u_ref, carry_in):
        reverse = d == 1
        t_len = cv_ref.shape[0]
        ls, stride = _seg_geometry(t_len)
        piece = min(tc, ls)
        n_lt = w // LANES
        wa, wi, ba, bi, lam = wa_ref[d], wi_ref[d], ba_ref[d], bi_ref[d], lam_ref[d]
        nsp = -LRU_C * (jnp.maximum(-lam, 0.0) + jnp.log(1.0 + jnp.exp(-jnp.abs(lam))))
        c0 = (0.5 * math.log2(math.e)) * nsp

        def seg_row(t):
            return pl.multiple_of(t + SUBLANES * (t // ls), SUBLANES)

        def coeff_body(ci, _):
            t0 = pl.multiple_of(ci * tc, tc)
            x = cv_ref[pl.ds(t0, tc), :]
            xb = x.astype(BF16)
            tr = jnp.tanh(jnp.dot(xb, wa, preferred_element_type=F32) + ba)
            ti = jnp.tanh(jnp.dot(xb, wi, preferred_element_type=F32) + bi)
            a = jnp.exp2(c0 + c0 * tr)
            u = jnp.sqrt(1.0 - a * a) * ((0.5 + 0.5 * ti) * x)
            for p in range(tc // piece):
                row = seg_row(t0 + p * piece)
                for lt in range(n_lt):
                    lanes = slice(lt * LANES, (lt + 1) * LANES)
                    a_ref[lt, pl.ds(row, piece), :] = a[p * piece:(p + 1) * piece, lanes]
                    u_ref[lt, pl.ds(row, piece), :] = u[p * piece:(p + 1) * piece, lanes]
            return 0

        lax.fori_loop(0, t_len // tc, coeff_body, 0, unroll=2 if (t_len // tc) % 2 == 0 else 1)

        def scan_body(k, carry):
            r = (ls - 1 - k) if reverse else k
            idx = pl.ds(r, SUBLANES, stride=stride)
            out = []
            for lt in range(n_lt):
                h, p = carry[lt]
                a = a_ref[lt, idx, :]
                h = a * h + u_ref[lt, idx, :]
                p = a * p
                u_ref[lt, idx, :] = h
                a_ref[lt, idx, :] = p
                out.append((h, p))
            return tuple(out)

        init = tuple((jnp.zeros((SUBLANES, LANES), F32), jnp.ones((SUBLANES, LANES), F32))
                     for _ in range(n_lt))
        ends = lax.fori_loop(0, ls, scan_body, init, unroll=8)
        h_end = jnp.concatenate([e[0] for e in ends], axis=1)
        p_end = jnp.concatenate([e[1] for e in ends], axis=1)

        c = carry_in
        seg_in = [None] * SUBLANES
        for s in (range(SUBLANES - 1, -1, -1) if reverse else range(SUBLANES)):
            seg_in[s] = c
            c = p_end[s:s + 1] * c + h_end[s:s + 1]
        for s in range(SUBLANES):
            for q in range(ls // piece):
                src = s * stride + q * piece
                dst = s * ls + q * piece
                for lt in range(n_lt):
                    lanes = slice(lt * LANES, (lt + 1) * LANES)
                    h = (u_ref[lt, src:src + piece, :]
                         + a_ref[lt, src:src + piece, :] * seg_in[s][:, lanes])
                    if d == 0:
                        acc_ref[dst:dst + piece, lanes] = h
                    else:
                        acc_ref[dst:dst + piece, lanes] = acc_ref[dst:dst + piece, lanes] + h
        return c

    for d in range(2):
        carry = run_seq(d, cvc_ref, accc_ref, ac_ref, uc_ref, jnp.zeros((1, w), F32))
        run_seq(d, cvl_ref, accl_ref, al_ref, ul_ref, carry)

    def finish(acc_ref, y_ref, o_ref):
        nchunk = acc_ref.shape[0] // tc

        def body(ci, _):
            t0 = pl.multiple_of(ci * tc, tc)
            o_ref[pl.ds(t0, tc), :] = (acc_ref[pl.ds(t0, tc), :]
                                       * y_ref[pl.ds(t0, tc), :].astype(F32)).astype(o_ref.dtype)
            return 0

        lax.fori_loop(0, nchunk, body, 0)

    finish(accl_ref, yl_ref, ol_ref)
    finish(accc_ref, yc_ref, oc_ref)


def lru_mix(xr, yg, layer, conv_w, conv_b, w_a, w_i, b_a, b_i, lam, *, s_len, c_len, n_batch, tc=256):
    r, rn = xr.shape
    nblk, bw = w_a.shape[2], w_a.shape[3]
    assert s_len % tc == 0 and c_len % tc == 0 and (n_batch * s_len) % c_len == 0
    ctx0 = n_batch * s_len // c_len
    lat = lambda b, k: (b, k)
    ctx = lambda b, k: (ctx0 + b, k)
    vec = lambda b, k: (layer, 0, 0, k)
    seg_rows = lambda t: SUBLANES * _seg_geometry(t)[1]
    seg_buf = lambda t: pltpu.VMEM((bw // LANES, seg_rows(t), LANES), F32)
    n_dir = w_a.shape[1]
    return pl.pallas_call(
        functools.partial(_lru_kernel, tc),
        grid=(n_batch, nblk),
        in_specs=[
            pl.BlockSpec((s_len, bw), lat), pl.BlockSpec((c_len, bw), ctx),
            pl.BlockSpec((s_len, bw), lat), pl.BlockSpec((c_len, bw), ctx),
            pl.BlockSpec((None, conv_w.shape[1], bw), lambda b, k: (layer, 0, k)),
            pl.BlockSpec((None, 1, bw), lambda b, k: (layer, 0, k)),
            pl.BlockSpec((None, n_dir, None, bw, bw), lambda b, k: (layer, 0, k, 0, 0)),
            pl.BlockSpec((None, n_dir, None, bw, bw), lambda b, k: (layer, 0, k, 0, 0)),
            pl.BlockSpec((None, n_dir, 1, bw), vec), pl.BlockSpec((None, n_dir, 1, bw), vec),
            pl.BlockSpec((None, n_dir, 1, bw), vec),
        ],
        out_specs=[pl.BlockSpec((s_len, bw), lambda b, k: (b, k)),
                   pl.BlockSpec((c_len, bw), lambda b, k: (b, k))],
        out_shape=[jax.ShapeDtypeStruct((n_batch * s_len, rn), BF16),
                   jax.ShapeDtypeStruct((n_batch * c_len, rn), BF16)],
        scratch_shapes=[pltpu.VMEM((s_len, bw), F32), pltpu.VMEM((c_len, bw), F32),
                        pltpu.VMEM((s_len, bw), F32), pltpu.VMEM((c_len, bw), F32),
                        seg_buf(s_len), seg_buf(s_len), seg_buf(c_len), seg_buf(c_len)],
        compiler_params=_params(("arbitrary", "arbitrary"),
                                (s_len + c_len) * bw * (2 * 4 + 2 * 2 + 2 * 2 + 8)
                                + 2 * (seg_rows(s_len) + seg_rows(c_len)) * bw * 4
                                + 8 * bw * bw + 24 * tc * bw * 4),
        name="lru_mix",
    )(xr, xr, yg, yg, conv_w, conv_b.reshape(conv_b.shape[0], 1, rn),
      (0.5 * w_a).astype(BF16), (0.5 * w_i).astype(BF16),
      (0.5 * b_a).reshape(-1, n_dir, 1, rn), (0.5 * b_i).reshape(-1, n_dir, 1, rn),
      lam.reshape(-1, n_dir, 1, rn))


_DN_NT = (((1,), (1,)), ((), ()))


def _split_heads(q, groups, hd):
    return jnp.concatenate([q[:, g * hd:(g + 1) * hd] for g in range(groups)], axis=0)


def _attn_pass1(q4, k_refs, s_ref, m_ref, kc):
    m_rows = q4.shape[0]
    mx = jnp.full((m_rows, LANES), -jnp.inf, F32)
    off = 0
    for k_ref in k_refs:
        for c0 in range(0, k_ref.shape[0], kc):
            w = min(kc, k_ref.shape[0] - c0)
            s = lax.dot_general(q4, k_ref[c0:c0 + w, :], _DN_NT, preferred_element_type=F32)
            s_ref[:, off:off + w] = s
            for t in range(w // LANES):
                mx = jnp.maximum(mx, s[:, t * LANES:(t + 1) * LANES])
            off += w
    m_ref[...] = jnp.broadcast_to(jnp.max(mx, axis=-1, keepdims=True), (m_rows, LANES))


def _attn_pass2(v_refs, s_ref, m_ref, kc):
    m_rows = s_ref.shape[0]
    hd = v_refs[0].shape[1]
    mrow = m_ref[...]
    lsum = jnp.zeros((m_rows, LANES), F32)
    o = jnp.zeros((m_rows, hd), F32)
    off = 0
    for v_ref in v_refs:
        for c0 in range(0, v_ref.shape[0], kc):
            w = min(kc, v_ref.shape[0] - c0)
            ps = []
            for t in range(w // LANES):
                p = jnp.exp2(s_ref[:, off + t * LANES:off + (t + 1) * LANES] - mrow)
                lsum = lsum + p
                ps.append(p.astype(BF16))
            p_all = jnp.concatenate(ps, axis=1) if len(ps) > 1 else ps[0]
            o = o + jnp.dot(p_all, v_ref[c0:c0 + w, :], preferred_element_type=F32)
            off += w
    return o / jnp.sum(lsum, axis=-1, keepdims=True)


def _emit_heads(o, o_ref, row0, tq, groups, hd):
    for g in range(groups):
        o_ref[row0:row0 + tq, g * hd:(g + 1) * hd] = o[g * tq:(g + 1) * tq].astype(o_ref.dtype)


def _attn_lat_kernel(tq, groups, hd, kc, qc_ref, qn_ref, kl_ref, vl_ref, kcx_ref, vcx_ref, o_ref,
                     s0_ref, s1_ref, m0_ref, m1_ref):
    keys = [kcx_ref, kl_ref]
    vals = [vcx_ref, vl_ref]

    @pl.when(pl.program_id(2) == 0)
    def _():
        _attn_pass1(_split_heads(qc_ref[0:tq, :], groups, hd), keys, s0_ref, m0_ref, kc)

    _attn_pass1(_split_heads(qc_ref[tq:2 * tq, :], groups, hd), keys, s1_ref, m1_ref, kc)
    _emit_heads(_attn_pass2(vals, s0_ref, m0_ref, kc), o_ref, 0, tq, groups, hd)
    _attn_pass1(_split_heads(qn_ref[0:tq, :], groups, hd), keys, s0_ref, m0_ref, kc)
    _emit_heads(_attn_pass2(vals, s1_ref, m1_ref, kc), o_ref, tq, tq, groups, hd)


def _attn_ctx_kernel(tq, groups, hd, kc, q_ref, k_ref, v_ref, o_ref, s_ref, m_ref):
    _attn_pass1(_split_heads(q_ref[...], groups, hd), [k_ref], s_ref, m_ref, kc)
    _emit_heads(_attn_pass2([v_ref], s_ref, m_ref, kc), o_ref, 0, tq, groups, hd)


def attention(qkv, *, n_heads, n_kv, hd, s_len, c_len, n_batch, d_out, with_ctx=True, tq=128, kc=512):
    groups = n_heads // n_kv
    gw = groups * hd
    m_rows = groups * tq
    assert s_len % (2 * tq) == 0 and c_len % tq == 0 and (n_batch * s_len) % c_len == 0
    n_pair = s_len // (2 * tq)
    ctx_k0 = n_batch * s_len // c_len
    kv_bytes = 4 * (s_len + c_len) * hd * 2
    o_lat = pl.pallas_call(
        functools.partial(_attn_lat_kernel, tq, groups, hd, kc),
        grid=(n_batch, n_kv, n_pair),
        in_specs=[
            pl.BlockSpec((2 * tq, gw), lambda b, kv, i: (b * n_pair + i, kv)),
            pl.BlockSpec((2 * tq, gw), lambda b, kv, i: (b * n_pair + jnp.minimum(i + 1, n_pair - 1), kv)),
            pl.BlockSpec((s_len, hd), lambda b, kv, i: (b, n_heads + kv)),
            pl.BlockSpec((s_len, hd), lambda b, kv, i: (b, n_heads + n_kv + kv)),
            pl.BlockSpec((c_len, hd), lambda b, kv, i: (ctx_k0 + b, n_heads + kv)),
            pl.BlockSpec((c_len, hd), lambda b, kv, i: (ctx_k0 + b, n_heads + n_kv + kv)),
        ],
        out_specs=pl.BlockSpec((2 * tq, gw), lambda b, kv, i: (b * n_pair + i, kv)),
        out_shape=jax.ShapeDtypeStruct((n_batch * s_len, d_out), BF16),
        scratch_shapes=[pltpu.VMEM((m_rows, s_len + c_len), F32), pltpu.VMEM((m_rows, s_len + c_len), F32),
                        pltpu.VMEM((m_rows, LANES), F32), pltpu.VMEM((m_rows, LANES), F32)],
        compiler_params=_params(("arbitrary", "arbitrary", "arbitrary"),
                                kv_bytes + 12 * tq * gw * 2 + 2 * m_rows * (s_len + c_len) * 4
                                + 8 * m_rows * kc * 4),
        name="attention_lat",
    )(qkv, qkv, qkv, qkv, qkv, qkv)

    if not with_ctx:
        return o_lat, None
    nq_ctx = c_len // tq
    ctx_q0 = n_batch * s_len // tq
    o_ctx = pl.pallas_call(
        functools.partial(_attn_ctx_kernel, tq, groups, hd, kc),
        grid=(n_batch, n_kv, nq_ctx),
        in_specs=[
            pl.BlockSpec((tq, gw), lambda b, kv, i: (ctx_q0 + b * nq_ctx + i, kv)),
            pl.BlockSpec((c_len, hd), lambda b, kv, i: (ctx_k0 + b, n_heads + kv)),
            pl.BlockSpec((c_len, hd), lambda b, kv, i: (ctx_k0 + b, n_heads + n_kv + kv)),
        ],
        out_specs=pl.BlockSpec((tq, gw), lambda b, kv, i: (b * nq_ctx + i, kv)),
        out_shape=jax.ShapeDtypeStruct((n_batch * c_len, d_out), BF16),
        scratch_shapes=[pltpu.VMEM((m_rows, c_len), F32), pltpu.VMEM((m_rows, LANES), F32)],
        compiler_params=_params(("arbitrary", "arbitrary", "arbitrary"),
                                4 * c_len * hd * 2 + 4 * tq * gw * 2 + m_rows * c_len * 4
                                + 8 * m_rows * kc * 4),
        name="attention_ctx",
    )(qkv, qkv, qkv)
    return o_lat, o_ctx


def _moe_kernel(layer, tm, f, te_ref, tv_ref, nx_ref, nv_ref, src_ref, h_hbm, w1_hbm, w3_hbm, w2_hbm,
                y_ref, xbuf, st1, st3, st2, w13b, w2b, sem_x, sem_w):
    i = pl.program_id(0)
    slot = i % 2
    valid = tv_ref[i] == 1

    def row_copy(t, r, sl):
        row = src_ref[t * tm + r]
        return pltpu.make_async_copy(h_hbm.at[pl.ds(row, 1)], xbuf.at[sl, pl.ds(r, 1)], sem_x.at[sl])

    def wait_rows(sl):
        pltpu.make_async_copy(h_hbm.at[pl.ds(0, tm)], xbuf.at[sl], sem_x.at[sl]).wait()

    def weight_copies(e):
        return (pltpu.make_async_copy(w1_hbm.at[layer, e], st1, sem_w.at[0]),
                pltpu.make_async_copy(w3_hbm.at[layer, e], st3, sem_w.at[1]),
                pltpu.make_async_copy(w2_hbm.at[layer, e], st2, sem_w.at[2]))

    @pl.when(jnp.logical_and(i == 0, valid))
    def _():
        def body(r, _):
            row_copy(0, r, 0).start()
            return 0
        lax.fori_loop(0, tm, body, 0, unroll=8)
        for cp in weight_copies(te_ref[0]):
            cp.start(priority=1)

    new_expert = jnp.logical_or(i == 0, te_ref[i] != te_ref[jnp.maximum(i - 1, 0)])

    @pl.when(jnp.logical_and(new_expert, valid))
    def _():
        for cp in weight_copies(te_ref[i]):
            cp.wait()
        w13b[:, :f] = st1[...].astype(BF16)
        w13b[:, f:] = st3[...].astype(BF16)
        w2b[...] = st2[...].astype(BF16)

        @pl.when(nx_ref[i] >= 0)
        def _():
            for cp in weight_copies(nx_ref[i]):
                cp.start(priority=1)

    @pl.when(valid)
    def _():
        wait_rows(slot)
        x = _unpack_bf16_pairs(xbuf[slot])
        h13 = jnp.dot(x, w13b[...], preferred_element_type=F32)
        h1 = h13[:, :f]
        act = ((h1 * _sigmoid(h1)) * h13[:, f:]).astype(BF16)
        for r in range(tm):
            row_copy(i + 1, r, 1 - slot).start()
        y_ref[...] = _pack_bf16_pairs(jnp.dot(act, w2b[...], preferred_element_type=F32))

    @pl.when(jnp.logical_not(valid))
    def _():
        y_ref[...] = jnp.zeros_like(y_ref)

        @pl.when(i == nv_ref[0])
        def _():
            wait_rows(slot)


def moe_experts(hp, tile_expert, tile_valid, next_expert, n_valid, src_rows, w1, w3, w2, layer, *, tm):
    half = hp.shape[1]
    d = 2 * half
    p = src_rows.shape[0]
    f = w2.shape[2]
    nt = p // tm
    any_spec = pl.BlockSpec(memory_space=pl.ANY)
    grid_spec = pltpu.PrefetchScalarGridSpec(
        num_scalar_prefetch=5,
        grid=(nt,),
        in_specs=[any_spec, any_spec, any_spec, any_spec],
        out_specs=pl.BlockSpec((tm, half), lambda i, te, tv, nx, nv, src: (i, 0)),
        scratch_shapes=[pltpu.VMEM((2, tm, half), jnp.uint32),
                        pltpu.VMEM((d, f), F32), pltpu.VMEM((d, f), F32), pltpu.VMEM((f, d), F32),
                        pltpu.VMEM((d, 2 * f), BF16), pltpu.VMEM((f, d), BF16),
                        pltpu.SemaphoreType.DMA((2,)), pltpu.SemaphoreType.DMA((3,))],
    )
    return pl.pallas_call(
        functools.partial(_moe_kernel, layer, tm, f),
        grid_spec=grid_spec,
        out_shape=jax.ShapeDtypeStruct((p, half), jnp.uint32),
        compiler_params=_params(("arbitrary",),
                                2 * tm * half * 4 + 3 * d * f * 4 + 3 * d * f * 2 + 2 * tm * d * 4
                                + tm * d * 6 + tm * 2 * f * 8),
        name="moe_experts",
    )(tile_expert, tile_valid, next_expert, n_valid, src_rows, hp, w1, w3, w2)


def _combine_kernel(tm, mode, pos_ref, x_ref, g_ref, r_ref, y_hbm, *rest):
    if mode == "mid":
        gs_ref, sh_ref, o_ref, h_ref, ybuf, sem = rest
    else:
        gs_ref, o_ref, ybuf, sem = rest
    i = pl.program_id(0)
    nt = pl.num_programs(0)
    slot = i % 2

    def issue(t, sl):
        def body(r, _):
            for k in range(2):
                row = pos_ref[2 * (t * tm + r) + k]
                pltpu.make_async_copy(y_hbm.at[pl.ds(row, 1)], ybuf.at[sl, k, pl.ds(r, 1)],
                                      sem.at[sl]).start()
            return 0
        lax.fori_loop(0, tm, body, 0, unroll=4)

    @pl.when(i == 0)
    def _():
        issue(0, 0)

    @pl.when(i + 1 < nt)
    def _():
        issue(i + 1, 1 - slot)

    for k in range(2):
        pltpu.make_async_copy(y_hbm.at[pl.ds(0, tm)], ybuf.at[slot, k], sem.at[slot]).wait()
    rinfo = r_ref[...]
    moe = (rinfo[:, 2:3] * _unpack_bf16_pairs(ybuf[slot, 0], F32)
           + rinfo[:, 3:4] * _unpack_bf16_pairs(ybuf[slot, 1], F32))
    x_new = x_ref[...] + g_ref[...] * moe
    if mode == "mid":
        o_ref[...] = x_new
        h_ref[...] = _modulate(x_new, gs_ref[...], sh_ref[...]).astype(h_ref.dtype)
    else:
        ms = jnp.mean(x_new * x_new, axis=-1, keepdims=True)
        o_ref[...] = x_new * lax.rsqrt(ms + EPS) * gs_ref[...]


def moe_combine(x, gate, rinfo, y, pos, gs, sh, *, n_rows, s_len, n_batch, tm=128):
    d = x.shape[1]
    mode = "mid" if sh is not None else "last"
    assert n_rows % tm == 0 and n_rows <= x.shape[0]
    cls = _cls_of_tile(tm, s_len, s_len * n_batch, n_batch)
    row_spec = pl.BlockSpec((tm, d), lambda i, pos: (i, 0))
    cls_spec = pl.BlockSpec((None, 1, d), lambda i, pos: (cls(i), 0, 0))
    in_specs = [row_spec, cls_spec, pl.BlockSpec((tm, LANES), lambda i, pos: (i, 0)),
                pl.BlockSpec(memory_space=pl.ANY)]
    if mode == "mid":
        in_specs += [cls_spec, cls_spec]
        operands = (gs, sh)
        out_specs = [row_spec, row_spec]
        out_shape = [jax.ShapeDtypeStruct((n_rows, d), F32), jax.ShapeDtypeStruct((n_rows, d), BF16)]
    else:
        in_specs += [pl.BlockSpec((1, d), lambda i, pos: (0, 0))]
        operands = (gs.reshape(1, d),)
        out_specs = row_spec
        out_shape = jax.ShapeDtypeStruct((n_rows, d), F32)
    grid_spec = pltpu.PrefetchScalarGridSpec(
        num_scalar_prefetch=1,
        grid=(n_rows // tm,),
        in_specs=in_specs,
        out_specs=out_specs,
        scratch_shapes=[pltpu.VMEM((2, 2, tm, d // 2), jnp.uint32), pltpu.SemaphoreType.DMA((2,))],
    )
    return pl.pallas_call(
        functools.partial(_combine_kernel, tm, mode),
        grid_spec=grid_spec,
        out_shape=out_shape,
        compiler_params=_params(("arbitrary",), 4 * tm * d * 4 + 6 * tm * d * 4 + 4 * tm * d * 4),
        name="moe_combine_" + mode,
    )(pos, x, gate, rinfo, y, *operands)


def _route_plan(rinfo, n_experts, tm):
    r = rinfo.shape[0]
    e = rinfo[:, 0:2].astype(jnp.int32).reshape(-1)
    n_assign = 2 * r
    onehot = (e[:, None] == jnp.arange(n_experts, dtype=jnp.int32)[None, :]).astype(jnp.int32)
    csum = jnp.cumsum(onehot, axis=0)
    rank = jnp.sum(csum * onehot, axis=1) - 1
    counts = csum[-1]
    ntile = (counts + tm - 1) // tm
    tend = jnp.cumsum(ntile)
    tstart = tend - ntile
    pos = tstart[e] * tm + rank
    n_tiles = n_assign // tm + n_experts + 1
    p = n_tiles * tm
    src = jnp.zeros((p,), jnp.int32).at[pos].set(jnp.arange(n_assign, dtype=jnp.int32) // 2)
    tid = jnp.arange(n_tiles, dtype=jnp.int32)
    tile_valid = (tid < tend[-1]).astype(jnp.int32)
    tile_expert = jnp.minimum(jnp.sum((tid[:, None] >= tend[None, :]).astype(jnp.int32), axis=1),
                              n_experts - 1)
    nxt = tend[tile_expert]
    next_expert = jnp.where(nxt < tend[-1], tile_expert[jnp.minimum(nxt, n_tiles - 1)], -1)
    last = jnp.maximum(tend[-1] - 1, 0)
    tile_expert = jnp.where(tile_valid == 1, tile_expert, tile_expert[last])
    return pos, src, tile_expert, tile_valid, next_expert.astype(jnp.int32), tend[-1:].astype(jnp.int32)


def _rope_tables(s_len, hd, tm):
    rows = s_len // ROPE_GRID_W
    t_row = jnp.repeat(jnp.arange(rows), ROPE_GRID_W).astype(F32)
    t_col = jnp.tile(jnp.arange(ROPE_GRID_W), rows).astype(F32)
    n_f = hd // 4
    inv = ROPE_THETA ** (-jnp.arange(n_f, dtype=F32) / n_f)
    ang = jnp.concatenate([t_row[:, None] * inv, t_col[:, None] * inv], axis=-1)
    cos, sin = jnp.cos(ang), jnp.sin(ang)
    cos2 = jnp.concatenate([cos, cos], axis=-1)
    sin2 = jnp.concatenate([-sin, sin], axis=-1)
    cos2 = jnp.concatenate([cos2, jnp.ones((tm, hd), F32)], axis=0)
    sin2 = jnp.concatenate([sin2, jnp.zeros((tm, hd), F32)], axis=0)
    return cos2, sin2


def kernel(x, c, ctx, c_ctx, ada_w, ada_b, norm_mix, norm_ffn, final_norm_g, lru_w_in, lru_conv_w,
           lru_conv_b, lru_w_a, lru_b_a, lru_w_i, lru_b_i, lru_lam, lru_w_out, attn_w_qkv, attn_q_norm,
           attn_k_norm, attn_w_o, moe_w_rg, moe_b_rg, moe_w_re, moe_b_re, moe_w1, moe_w3, moe_w2):
    n_batch, s_len, d = x.shape
    c_len = ctx.shape[1]
    depth = ada_w.shape[0]
    n_ada = ada_w.shape[2] // d
    d_rnn = lru_w_in.shape[2] // 2
    hd = attn_q_norm.shape[1]
    n_heads = d // hd
    n_kv = (attn_w_qkv.shape[2] // hd - n_heads) // 2
    n_groups = moe_w_rg.shape[2]
    n_experts = moe_w_re.shape[2]
    per_group = n_experts // n_groups
    n_lat = n_batch * s_len
    tm = 512 if (s_len % 512 == 0 and (n_batch * c_len) % 512 == 0) else 256
    moe_tm = 256
    geo = dict(s_len=s_len, n_batch=n_batch)
    assert n_batch + 1 <= NCLS_PAD and n_groups + n_experts <= LANES

    xs = jnp.concatenate([x.reshape(n_lat, d), ctx.reshape(n_batch * c_len, d)], axis=0)

    cond = jnp.zeros((NCLS_PAD, d), F32).at[:n_batch].set(c).at[n_batch].set(c_ctx)
    mods = ada_all(cond, ada_w, ada_b).reshape(depth, NCLS_PAD, n_ada, d)
    cos_tbl, sin_tbl = _rope_tables(s_len, hd, tm)

    def cls_vec(v):
        return v.reshape(NCLS_PAD, 1, d)

    n_mixers = 2
    n_all = xs.shape[0]

    def mix_mod(l):
        return cls_vec(norm_mix[l] * (1.0 + mods[l, :, 1])), cls_vec(mods[l, :, 0])

    h = modulate(xs, *mix_mod(0), **geo)
    for l in range(depth):
        last = l == depth - 1
        sh_m, sc_m, g_m, sh_f, sc_f, g_f = [mods[l, :, j] for j in range(n_ada)]
        j = l // n_mixers
        if l % n_mixers == 0:
            yg = matmul(h, lru_w_in, j, n_cols=d_rnn, col0=0, out_dtype=BF16, act="gelu", tm=tm)
            xr = matmul(h, lru_w_in, j, n_cols=d_rnn, col0=d_rnn, out_dtype=F32, tm=tm)
            z_lat, z_ctx = lru_mix(xr, yg, j, lru_conv_w, lru_conv_b, lru_w_a, lru_w_i, lru_b_a,
                                   lru_b_i, lru_lam, c_len=c_len, **geo)
            xs = matmul_resid(z_lat, lru_w_out, j, xs, cls_vec(g_m),
                              a_tail=None if last else z_ctx, tm=tm, **geo)
        else:
            q_scale = hd ** -0.5 * math.log2(math.e)
            gain = jnp.concatenate([jnp.tile(attn_q_norm[j] * q_scale, n_heads),
                                    jnp.tile(attn_k_norm[j], n_kv),
                                    jnp.ones((n_kv * hd,), F32)]).reshape(1, -1)
            qkv = matmul_qkv(h, attn_w_qkv, j, gain, cos_tbl, sin_tbl,
                             n_norm_cols=(n_heads + n_kv) * hd, hd=hd, tm=tm, **geo)
            o_lat, o_ctx = attention(qkv, n_heads=n_heads, n_kv=n_kv, hd=hd, c_len=c_len, d_out=d,
                                     with_ctx=not last, **geo)
            xs = matmul_resid(o_lat, attn_w_o, j, xs, cls_vec(g_m), a_tail=o_ctx, tm=tm, **geo)

        n_rows = n_lat if last else n_all
        wr = jnp.zeros((d, LANES), F32).at[:, :n_groups].set(moe_w_rg[l])
        wr = wr.at[:, n_groups:n_groups + n_experts].set(moe_w_re[l])
        br = jnp.zeros((1, LANES), F32).at[0, :n_groups].set(moe_b_rg[l])
        br = br.at[0, n_groups:n_groups + n_experts].set(moe_b_re[l])
        wr_hi = wr.astype(BF16)
        wr = jnp.concatenate([wr_hi, (wr - wr_hi.astype(F32)).astype(BF16)], axis=1)
        hp, rinfo = modulate_router(xs, cls_vec(norm_ffn[l] * (1.0 + sc_f)), cls_vec(sh_f), wr, br,
                                    n_rows=n_rows, n_groups=n_groups, per_group=per_group, **geo)
        pos, src, tile_expert, tile_valid, next_expert, n_valid = _route_plan(rinfo, n_experts, moe_tm)
        y = moe_experts(hp, tile_expert, tile_valid, next_expert, n_valid, src, moe_w1, moe_w3, moe_w2,
                        l, tm=moe_tm)
        if last:
            out = moe_combine(xs, cls_vec(g_f), rinfo, y, pos, final_norm_g, None, n_rows=n_rows, **geo)
        else:
            xs, h = moe_combine(xs, cls_vec(g_f), rinfo, y, pos, *mix_mod(l + 1), n_rows=n_rows, **geo)

    return out.reshape(n_batch, s_len, d)
```

```python
import functools
import math

import jax
import jax.numpy as jnp
from jax import lax
from jax.experimental import pallas as pl
from jax.experimental.pallas import tpu as pltpu

EPS = 1e-6
LRU_C = 8.0
ROPE_THETA = 10000.0
ROPE_GRID_W = 64
LANES = 128
SUBLANES = 8
NCLS_PAD = 8
V7X_VMEM_CAP = 56 * 1024 * 1024
MOE_LOOKAHEAD = 2
MOE_SLOTS = MOE_LOOKAHEAD + 1
BF16 = jnp.bfloat16
F32 = jnp.float32


def _vmem_limit(nbytes):
    return int(min(V7X_VMEM_CAP, max(16 * 1024 * 1024, nbytes * 5 // 4 + (4 << 20))))


def _params(sem, nbytes):
    return pltpu.CompilerParams(dimension_semantics=sem, vmem_limit_bytes=_vmem_limit(nbytes))


def _sigmoid(x):
    return 1.0 / (1.0 + jnp.exp(-x))


def _gelu_tanh(x):
    return 0.5 * x * (1.0 + jnp.tanh(math.sqrt(2.0 / math.pi) * (x + 0.044715 * (x * x * x))))


def _ada_kernel(c_ref, w_ref, b_ref, o_ref):
    c = c_ref[...]
    s = (c * _sigmoid(c)).astype(BF16)
    w = w_ref[...].astype(BF16)
    o_ref[...] = jnp.dot(s, w, preferred_element_type=F32) + b_ref[...]


def ada_all(cond, ada_w, ada_b):
    depth, d, n = ada_w.shape
    tn = 512
    return pl.pallas_call(
        _ada_kernel,
        grid=(depth, n // tn),
        in_specs=[
            pl.BlockSpec((NCLS_PAD, d), lambda l, j: (0, 0)),
            pl.BlockSpec((None, d, tn), lambda l, j: (l, 0, j)),
            pl.BlockSpec((None, 1, tn), lambda l, j: (l, 0, j)),
        ],
        out_specs=pl.BlockSpec((None, NCLS_PAD, tn), lambda l, j: (l, 0, j)),
        out_shape=jax.ShapeDtypeStruct((depth, NCLS_PAD, n), F32),
        compiler_params=_params(("arbitrary", "arbitrary"), 2 * d * tn * 4 + d * tn * 2),
        name="ada_all",
    )(cond, ada_w, ada_b.reshape(depth, 1, n))


def _modulate(x, gs, sh):
    ms = jnp.mean(x * x, axis=-1, keepdims=True)
    return x * lax.rsqrt(ms + EPS) * gs + sh


def _mod_kernel(x_ref, gs_ref, sh_ref, o_ref):
    o_ref[...] = _modulate(x_ref[...], gs_ref[...], sh_ref[...]).astype(o_ref.dtype)


def _pack_bf16_pairs(h):
    half = h.shape[1] // 2
    bits = lax.bitcast_convert_type(h.astype(BF16).astype(F32), jnp.uint32)
    return (bits[:, :half] >> 16) | (bits[:, half:] & jnp.uint32(0xFFFF0000))


def _unpack_bf16_pairs(words, dtype=BF16):
    lo = lax.bitcast_convert_type(words << 16, F32).astype(dtype)
    hi = lax.bitcast_convert_type(words & jnp.uint32(0xFFFF0000), F32).astype(dtype)
    return jnp.concatenate([lo, hi], axis=1)


def _mod_router_kernel(n_groups, per_group, x_ref, gs_ref, sh_ref, wr_ref, br_ref, h_ref, r_ref):
    h = _modulate(x_ref[...], gs_ref[...], sh_ref[...])
    h_ref[...] = _pack_bf16_pairs(h)
    hh = h.astype(BF16)
    hl = (h - hh.astype(F32)).astype(BF16)
    r_hi = jnp.dot(hh, wr_ref[...], preferred_element_type=F32)
    r_lo = jnp.dot(hl, wr_ref[:, :LANES], preferred_element_type=F32)
    logits = r_hi[:, :LANES] + r_hi[:, LANES:] + r_lo + br_ref[...]
    lane = lax.broadcasted_iota(jnp.int32, logits.shape, 1)
    neg = jnp.float32(-jnp.inf)
    big = jnp.int32(1 << 20)
    gl = jnp.where(lane < n_groups, logits, neg)
    gmax = jnp.max(gl, axis=-1, keepdims=True)
    gsum = jnp.sum(jnp.exp(gl - gmax), axis=-1, keepdims=True)
    g_top = 1.0 / gsum
    g_idx = jnp.min(jnp.where(gl == gmax, lane, big), axis=-1, keepdims=True)
    lo = n_groups + per_group * g_idx
    el = jnp.where((lane >= lo) & (lane < lo + per_group), logits, neg)
    m1 = jnp.max(el, axis=-1, keepdims=True)
    i1 = jnp.min(jnp.where(el == m1, lane, big), axis=-1, keepdims=True)
    el2 = jnp.where(lane == i1, neg, el)
    m2 = jnp.max(el2, axis=-1, keepdims=True)
    i2 = jnp.min(jnp.where(el2 == m2, lane, big), axis=-1, keepdims=True)
    t = jnp.exp(m2 - m1)
    w1 = g_top / (1.0 + t)
    w2 = g_top * t / (1.0 + t)
    e1 = (i1 - n_groups).astype(F32)
    e2 = (i2 - n_groups).astype(F32)
    r_ref[...] = jnp.where(lane == 0, e1, jnp.where(lane == 1, e2,
                           jnp.where(lane == 2, w1, jnp.where(lane == 3, w2, 0.0))))


def _cls_of_tile(tm, s_len, n_lat, n_batch):
    def f(i):
        r0 = i * tm
        return jnp.where(r0 < n_lat, r0 // s_len, n_batch)
    return f


def modulate(x, gs, sh, *, s_len, n_batch, tm=256):
    r, d = x.shape
    cls = _cls_of_tile(tm, s_len, s_len * n_batch, n_batch)
    return pl.pallas_call(
        _mod_kernel,
        grid=(r // tm,),
        in_specs=[
            pl.BlockSpec((tm, d), lambda i: (i, 0)),
            pl.BlockSpec((None, 1, d), lambda i: (cls(i), 0, 0)),
            pl.BlockSpec((None, 1, d), lambda i: (cls(i), 0, 0)),
        ],
        out_specs=pl.BlockSpec((tm, d), lambda i: (i, 0)),
        out_shape=jax.ShapeDtypeStruct((r, d), BF16),
        compiler_params=_params(("arbitrary",), 2 * tm * d * 6),
        name="modulate",
    )(x, gs, sh)


def modulate_router(x, gs, sh, wr, br, *, n_rows, n_groups, per_group, s_len, n_batch, tm=256):
    r, d = n_rows, x.shape[1]
    assert r % tm == 0 and r <= x.shape[0]
    cls = _cls_of_tile(tm, s_len, s_len * n_batch, n_batch)
    return pl.pallas_call(
        functools.partial(_mod_router_kernel, n_groups, per_group),
        grid=(r // tm,),
        in_specs=[
            pl.BlockSpec((tm, d), lambda i: (i, 0)),
            pl.BlockSpec((None, 1, d), lambda i: (cls(i), 0, 0)),
            pl.BlockSpec((None, 1, d), lambda i: (cls(i), 0, 0)),
            pl.BlockSpec((d, 2 * LANES), lambda i: (0, 0)),
            pl.BlockSpec((1, LANES), lambda i: (0, 0)),
        ],
        out_specs=[
            pl.BlockSpec((tm, d // 2), lambda i: (i, 0)),
            pl.BlockSpec((tm, LANES), lambda i: (i, 0)),
        ],
        out_shape=[jax.ShapeDtypeStruct((r, d // 2), jnp.uint32),
                   jax.ShapeDtypeStruct((r, LANES), F32)],
        compiler_params=_params(("arbitrary",), 2 * tm * d * 8 + 2 * d * LANES * 4 + 4 * tm * d * 4),
        name="modulate_router",
    )(x, gs, sh, wr, br)


def _cast_weight_once(w_ref, wbf_ref):
    @pl.when(pl.program_id(1) == 0)
    def _():
        wbf_ref[...] = w_ref[...].astype(BF16)


def _mm_plain_kernel(act, a_ref, w_ref, o_ref, wbf_ref):
    _cast_weight_once(w_ref, wbf_ref)
    acc = jnp.dot(a_ref[...], wbf_ref[...], preferred_element_type=F32)
    if act == "gelu":
        acc = _gelu_tanh(acc)
    o_ref[...] = acc.astype(o_ref.dtype)


def _mm_resid_kernel(n_main_tiles, a_ref, a2_ref, w_ref, x_ref, g_ref, o_ref, wbf_ref):
    _cast_weight_once(w_ref, wbf_ref)
    i = pl.program_id(1)

    def emit(src_ref):
        acc = jnp.dot(src_ref[...], wbf_ref[...], preferred_element_type=F32)
        o_ref[...] = x_ref[...] + g_ref[...] * acc

    @pl.when(i < n_main_tiles)
    def _():
        emit(a_ref)

    @pl.when(i >= n_main_tiles)
    def _():
        emit(a2_ref)


def _mm_qkv_kernel(n_norm_tiles, hd, a_ref, w_ref, gain_ref, cos_ref, sin_ref, o_ref, wbf_ref):
    _cast_weight_once(w_ref, wbf_ref)
    acc = jnp.dot(a_ref[...], wbf_ref[...], preferred_element_type=F32)
    j = pl.program_id(0)

    @pl.when(j < n_norm_tiles)
    def _():
        cos = cos_ref[...]
        sin = sin_ref[...]
        for h in range(acc.shape[1] // hd):
            xh = acc[:, h * hd:(h + 1) * hd]
            ms = jnp.mean(xh * xh, axis=-1, keepdims=True)
            y = xh * lax.rsqrt(ms + EPS) * gain_ref[:, h * hd:(h + 1) * hd]
            y = y * cos + pltpu.roll(y, hd // 2, 1) * sin
            o_ref[:, h * hd:(h + 1) * hd] = y.astype(o_ref.dtype)

    @pl.when(j >= n_norm_tiles)
    def _():
        o_ref[...] = acc.astype(o_ref.dtype)


def _w_spec(k, tn, layer, col0):
    assert col0 % tn == 0
    index_map = lambda j, i: (layer, 0, col0 // tn + j)
    if _w_buffers(k, tn) == 1:
        return pl.BlockSpec((None, k, tn), index_map, pipeline_mode=pl.Buffered(1))
    return pl.BlockSpec((None, k, tn), index_map)


def _w_buffers(k, tn):
    return 1 if k * tn * 4 > 8 * 1024 * 1024 else 2


def _mm_vmem(tm, k, tn, out_bytes, extra=0):
    return (2 * (tm * k * 2 + tm * tn * out_bytes) + _w_buffers(k, tn) * k * tn * 4 + k * tn * 2
            + 2 * tm * tn * 4 + extra)


def matmul(a, w, layer, *, n_cols, col0=0, out_dtype, act=None, tm=512, tn=1024):
    r, k = a.shape
    tn = min(tn, n_cols)
    assert r % tm == 0 and n_cols % tn == 0
    ob = jnp.dtype(out_dtype).itemsize
    return pl.pallas_call(
        functools.partial(_mm_plain_kernel, act),
        grid=(n_cols // tn, r // tm),
        in_specs=[pl.BlockSpec((tm, k), lambda j, i: (i, 0)), _w_spec(k, tn, layer, col0)],
        out_specs=pl.BlockSpec((tm, tn), lambda j, i: (i, j)),
        out_shape=jax.ShapeDtypeStruct((r, n_cols), out_dtype),
        scratch_shapes=[pltpu.VMEM((k, tn), BF16)],
        compiler_params=_params(("arbitrary", "arbitrary"), _mm_vmem(tm, k, tn, ob)),
        name="matmul_" + (act or "plain"),
    )(a, w)


def matmul_resid(a, w, layer, xres, gate, *, s_len, n_batch, a_tail=None, tm=512, tn=1024):
    n = xres.shape[1]
    k = a.shape[1]
    tn = min(tn, n)
    n_main = a.shape[0] // tm
    if a_tail is None:
        a_tail, n_tail, r = a, 1, a.shape[0]
    else:
        n_tail, r = a_tail.shape[0] // tm, a.shape[0] + a_tail.shape[0]
    assert r % tm == 0 and n % tn == 0 and a.shape[0] % tm == 0 and r <= xres.shape[0]
    cls = _cls_of_tile(tm, s_len, s_len * n_batch, n_batch)
    return pl.pallas_call(
        functools.partial(_mm_resid_kernel, n_main),
        grid=(n // tn, r // tm),
        in_specs=[pl.BlockSpec((tm, k), lambda j, i: (jnp.minimum(i, n_main - 1), 0)),
                  pl.BlockSpec((tm, k), lambda j, i: (jnp.clip(i - n_main, 0, n_tail - 1), 0),
                               pipeline_mode=pl.Buffered(1)),
                  _w_spec(k, tn, layer, 0),
                  pl.BlockSpec((tm, tn), lambda j, i: (i, j)),
                  pl.BlockSpec((None, 1, tn), lambda j, i: (cls(i), 0, j))],
        out_specs=pl.BlockSpec((tm, tn), lambda j, i: (i, j)),
        out_shape=jax.ShapeDtypeStruct((r, n), F32),
        scratch_shapes=[pltpu.VMEM((k, tn), BF16)],
        compiler_params=_params(("arbitrary", "arbitrary"),
                                _mm_vmem(tm, k, tn, 4, tm * k * 2 + 2 * tm * tn * 4)),
        name="matmul_resid",
    )(a, a_tail, w, xres, gate)


def matmul_qkv(a, w, layer, gain, cos_tbl, sin_tbl, *, n_norm_cols, hd, s_len, n_batch, tm=512, tn=1024):
    r, k = a.shape
    n = w.shape[2]
    while n_norm_cols % tn or n % tn:
        tn //= 2
    assert tn % hd == 0 and s_len % tm == 0
    n_lat = s_len * n_batch
    pos_blocks = s_len // tm

    def pos(i):
        return jnp.where(i * tm < n_lat, i % pos_blocks, pos_blocks)

    return pl.pallas_call(
        functools.partial(_mm_qkv_kernel, n_norm_cols // tn, hd),
        grid=(n // tn, r // tm),
        in_specs=[pl.BlockSpec((tm, k), lambda j, i: (i, 0)),
                  _w_spec(k, tn, layer, 0),
                  pl.BlockSpec((1, tn), lambda j, i: (0, j)),
                  pl.BlockSpec((tm, hd), lambda j, i: (pos(i), 0)),
                  pl.BlockSpec((tm, hd), lambda j, i: (pos(i), 0))],
        out_specs=pl.BlockSpec((tm, tn), lambda j, i: (i, j)),
        out_shape=jax.ShapeDtypeStruct((r, n), BF16),
        scratch_shapes=[pltpu.VMEM((k, tn), BF16)],
        compiler_params=_params(("arbitrary", "arbitrary"),
                                _mm_vmem(tm, k, tn, 2, 4 * tm * hd * 4 + tm * tn * 4)),
        name="matmul_qkv",
    )(a, w, gain, cos_tbl, sin_tbl)


def _seg_geometry(t_len):
    ls = t_len // SUBLANES
    assert ls % SUBLANES == 0
    return ls, ls + SUBLANES


def _lru_kernel(tc, xl_ref, xc_ref, yl_ref, yc_ref, cw_ref, cb_ref, wa_ref, wi_ref, ba_ref, bi_ref,
                lam_ref, ol_ref, oc_ref, cvl_ref, cvc_ref, accl_ref, accc_ref,
                al_ref, ul_ref, ac_ref, uc_ref):
    w = xl_ref.shape[1]
    cw = cw_ref[...]
    cb = cb_ref[...]

    def conv_seq(x_ref, cv_ref):
        t_len = x_ref.shape[0]
        nchunk = t_len // tc

        def body(ci, _):
            t0 = pl.multiple_of(ci * tc, tc)
            main = x_ref[pl.ds(t0, tc), :]
            p0 = pl.multiple_of(jnp.maximum(t0 - 8, 0), 8)
            n0 = pl.multiple_of(jnp.minimum(t0 + tc, t_len - 8), 8)
            prev8 = jnp.where(ci > 0, x_ref[pl.ds(p0, 8), :], 0.0)
            next8 = jnp.where(ci < nchunk - 1, x_ref[pl.ds(n0, 8), :], 0.0)
            ext = jnp.concatenate([prev8, main, next8], axis=0)
            ne = tc + 16
            xm1 = pltpu.roll(ext, 1, 0)[8:8 + tc]
            xp1 = pltpu.roll(ext, ne - 1, 0)[8:8 + tc]
            xp2 = pltpu.roll(ext, ne - 2, 0)[8:8 + tc]
            cv_ref[pl.ds(t0, tc), :] = (cw[0:1] * xm1 + cw[1:2] * main + cw[2:3] * xp1
                                        + cw[3:4] * xp2 + cb)
            return 0

        lax.fori_loop(0, nchunk, body, 0)

    conv_seq(xl_ref, cvl_ref)
    conv_seq(xc_ref, cvc_ref)

    def run_seq(d, cv_ref, acc_ref, a_ref, u_ref, carry_in):
        reverse = d == 1
        t_len = cv_ref.shape[0]
        ls, stride = _seg_geometry(t_len)
        piece = min(tc, ls)
        n_lt = w // LANES
        wa, wi, ba, bi, lam = wa_ref[d], wi_ref[d], ba_ref[d], bi_ref[d], lam_ref[d]
        nsp = -LRU_C * (jnp.maximum(-lam, 0.0) + jnp.log(1.0 + jnp.exp(-jnp.abs(lam))))
        c0 = (0.5 * math.log2(math.e)) * nsp

        def seg_row(t):
            return pl.multiple_of(t + SUBLANES * (t // ls), SUBLANES)

        def coeff_body(ci, _):
            t0 = pl.multiple_of(ci * tc, tc)
            x = cv_ref[pl.ds(t0, tc), :]
            xb = x.astype(BF16)
            tr = jnp.tanh(jnp.dot(xb, wa, preferred_element_type=F32) + ba)
            ti = jnp.tanh(jnp.dot(xb, wi, preferred_element_type=F32) + bi)
            a = jnp.exp2(c0 + c0 * tr)
            u = jnp.sqrt(1.0 - a * a) * ((0.5 + 0.5 * ti) * x)
            for p in range(tc // piece):
                row = seg_row(t0 + p * piece)
                for lt in range(n_lt):
                    lanes = slice(lt * LANES, (lt + 1) * LANES)
                    a_ref[lt, pl.ds(row, piece), :] = a[p * piece:(p + 1) * piece, lanes]
                    u_ref[lt, pl.ds(row, piece), :] = u[p * piece:(p + 1) * piece, lanes]
            return 0

        lax.fori_loop(0, t_len // tc, coeff_body, 0, unroll=2 if (t_len // tc) % 2 == 0 else 1)

        def scan_body(k, carry):
            r = (ls - 1 - k) if reverse else k
            idx = pl.ds(r, SUBLANES, stride=stride)
            out = []
            for lt in range(n_lt):
                h, p = carry[lt]
                a = a_ref[lt, idx, :]
                h = a * h + u_ref[lt, idx, :]
                p = a * p
                u_ref[lt, idx, :] = h
                a_ref[lt, idx, :] = p
                out.append((h, p))
            return tuple(out)

        init = tuple((jnp.zeros((SUBLANES, LANES), F32), jnp.ones((SUBLANES, LANES), F32))
                     for _ in range(n_lt))
        ends = lax.fori_loop(0, ls, scan_body, init, unroll=8)
        h_end = jnp.concatenate([e[0] for e in ends], axis=1)
        p_end = jnp.concatenate([e[1] for e in ends], axis=1)

        c = carry_in
        seg_in = [None] * SUBLANES
        for s in (range(SUBLANES - 1, -1, -1) if reverse else range(SUBLANES)):
            seg_in[s] = c
            c = p_end[s:s + 1] * c + h_end[s:s + 1]
        for s in range(SUBLANES):
            for q in range(ls // piece):
                src = s * stride + q * piece
                dst = s * ls + q * piece
                for lt in range(n_lt):
                    lanes = slice(lt * LANES, (lt + 1) * LANES)
                    h = (u_ref[lt, src:src + piece, :]
                         + a_ref[lt, src:src + piece, :] * seg_in[s][:, lanes])
                    if d == 0:
                        acc_ref[dst:dst + piece, lanes] = h
                    else:
                        acc_ref[dst:dst + piece, lanes] = acc_ref[dst:dst + piece, lanes] + h
        return c

    for d in range(2):
        carry = run_seq(d, cvc_ref, accc_ref, ac_ref, uc_ref, jnp.zeros((1, w), F32))
        run_seq(d, cvl_ref, accl_ref, al_ref, ul_ref, carry)

    def finish(acc_ref, y_ref, o_ref):
        nchunk = acc_ref.shape[0] // tc

        def body(ci, _):
            t0 = pl.multiple_of(ci * tc, tc)
            o_ref[pl.ds(t0, tc), :] = (acc_ref[pl.ds(t0, tc), :]
                                       * y_ref[pl.ds(t0, tc), :].astype(F32)).astype(o_ref.dtype)
            return 0

        lax.fori_loop(0, nchunk, body, 0)

    finish(accl_ref, yl_ref, ol_ref)
    finish(accc_ref, yc_ref, oc_ref)


def lru_mix(xr, yg, layer, conv_w, conv_b, w_a, w_i, b_a, b_i, lam, *, s_len, c_len, n_batch, tc=256):
    r, rn = xr.shape
    nblk, bw = w_a.shape[2], w_a.shape[3]
    assert s_len % tc == 0 and c_len % tc == 0 and (n_batch * s_len) % c_len == 0
    ctx0 = n_batch * s_len // c_len
    lat = lambda b, k: (b, k)
    ctx = lambda b, k: (ctx0 + b, k)
    vec = lambda b, k: (layer, 0, 0, k)
    seg_rows = lambda t: SUBLANES * _seg_geometry(t)[1]
    seg_buf = lambda t: pltpu.VMEM((bw // LANES, seg_rows(t), LANES), F32)
    n_dir = w_a.shape[1]
    return pl.pallas_call(
        functools.partial(_lru_kernel, tc),
        grid=(n_batch, nblk),
        in_specs=[
            pl.BlockSpec((s_len, bw), lat), pl.BlockSpec((c_len, bw), ctx),
            pl.BlockSpec((s_len, bw), lat), pl.BlockSpec((c_len, bw), ctx),
            pl.BlockSpec((None, conv_w.shape[1], bw), lambda b, k: (layer, 0, k)),
            pl.BlockSpec((None, 1, bw), lambda b, k: (layer, 0, k)),
            pl.BlockSpec((None, n_dir, None, bw, bw), lambda b, k: (layer, 0, k, 0, 0)),
            pl.BlockSpec((None, n_dir, None, bw, bw), lambda b, k: (layer, 0, k, 0, 0)),
            pl.BlockSpec((None, n_dir, 1, bw), vec), pl.BlockSpec((None, n_dir, 1, bw), vec),
            pl.BlockSpec((None, n_dir, 1, bw), vec),
        ],
        out_specs=[pl.BlockSpec((s_len, bw), lambda b, k: (b, k)),
                   pl.BlockSpec((c_len, bw), lambda b, k: (b, k))],
        out_shape=[jax.ShapeDtypeStruct((n_batch * s_len, rn), BF16),
                   jax.ShapeDtypeStruct((n_batch * c_len, rn), BF16)],
        scratch_shapes=[pltpu.VMEM((s_len, bw), F32), pltpu.VMEM((c_len, bw), F32),
                        pltpu.VMEM((s_len, bw), F32), pltpu.VMEM((c_len, bw), F32),
                        seg_buf(s_len), seg_buf(s_len), seg_buf(c_len), seg_buf(c_len)],
        compiler_params=_params(("arbitrary", "arbitrary"),
                                (s_len + c_len) * bw * (2 * 4 + 2 * 2 + 2 * 2 + 8)
                                + 2 * (seg_rows(s_len) + seg_rows(c_len)) * bw * 4
                                + 8 * bw * bw + 24 * tc * bw * 4),
        name="lru_mix",
    )(xr, xr, yg, yg, conv_w, conv_b.reshape(conv_b.shape[0], 1, rn),
      (0.5 * w_a).astype(BF16), (0.5 * w_i).astype(BF16),
      (0.5 * b_a).reshape(-1, n_dir, 1, rn), (0.5 * b_i).reshape(-1, n_dir, 1, rn),
      lam.reshape(-1, n_dir, 1, rn))


_DN_NT = (((1,), (1,)), ((), ()))


def _split_heads(q, groups, hd):
    return jnp.concatenate([q[:, g * hd:(g + 1) * hd] for g in range(groups)], axis=0)


def _attn_pass1(q4, k_refs, s_ref, m_ref, kc):
    m_rows = q4.shape[0]
    mx = jnp.full((m_rows, LANES), -jnp.inf, F32)
    off = 0
    for k_ref in k_refs:
        for c0 in range(0, k_ref.shape[0], kc):
            w = min(kc, k_ref.shape[0] - c0)
            s = lax.dot_general(q4, k_ref[c0:c0 + w, :], _DN_NT, preferred_element_type=F32)
            s_ref[:, off:off + w] = s
            for t in range(w // LANES):
                mx = jnp.maximum(mx, s[:, t * LANES:(t + 1) * LANES])
            off += w
    m_ref[...] = jnp.broadcast_to(jnp.max(mx, axis=-1, keepdims=True), (m_rows, LANES))


def _attn_pass2(v_refs, s_ref, m_ref, kc):
    m_rows = s_ref.shape[0]
    hd = v_refs[0].shape[1]
    mrow = m_ref[...]
    lsum = jnp.zeros((m_rows, LANES), F32)
    o = jnp.zeros((m_rows, hd), F32)
    off = 0
    for v_ref in v_refs:
        for c0 in range(0, v_ref.shape[0], kc):
            w = min(kc, v_ref.shape[0] - c0)
            ps = []
            for t in range(w // LANES):
                p = jnp.exp2(s_ref[:, off + t * LANES:off + (t + 1) * LANES] - mrow)
                lsum = lsum + p
                ps.append(p.astype(BF16))
            p_all = jnp.concatenate(ps, axis=1) if len(ps) > 1 else ps[0]
            o = o + jnp.dot(p_all, v_ref[c0:c0 + w, :], preferred_element_type=F32)
            off += w
    return o / jnp.sum(lsum, axis=-1, keepdims=True)


def _emit_heads(o, o_ref, row0, tq, groups, hd):
    for g in range(groups):
        o_ref[row0:row0 + tq, g * hd:(g + 1) * hd] = o[g * tq:(g + 1) * tq].astype(o_ref.dtype)


def _attn_lat_kernel(tq, groups, hd, kc, qc_ref, qn_ref, kl_ref, vl_ref, kcx_ref, vcx_ref, o_ref,
                     s0_ref, s1_ref, m0_ref, m1_ref):
    keys = [kcx_ref, kl_ref]
    vals = [vcx_ref, vl_ref]

    @pl.when(pl.program_id(2) == 0)
    def _():
        _attn_pass1(_split_heads(qc_ref[0:tq, :], groups, hd), keys, s0_ref, m0_ref, kc)

    _attn_pass1(_split_heads(qc_ref[tq:2 * tq, :], groups, hd), keys, s1_ref, m1_ref, kc)
    _emit_heads(_attn_pass2(vals, s0_ref, m0_ref, kc), o_ref, 0, tq, groups, hd)
    _attn_pass1(_split_heads(qn_ref[0:tq, :], groups, hd), keys, s0_ref, m0_ref, kc)
    _emit_heads(_attn_pass2(vals, s1_ref, m1_ref, kc), o_ref, tq, tq, groups, hd)


def _attn_ctx_kernel(tq, groups, hd, kc, q_ref, k_ref, v_ref, o_ref, s_ref, m_ref):
    _attn_pass1(_split_heads(q_ref[...], groups, hd), [k_ref], s_ref, m_ref, kc)
    _emit_heads(_attn_pass2([v_ref], s_ref, m_ref, kc), o_ref, 0, tq, groups, hd)


def attention(qkv, *, n_heads, n_kv, hd, s_len, c_len, n_batch, d_out, with_ctx=True, tq=128, kc=512):
    groups = n_heads // n_kv
    gw = groups * hd
    m_rows = groups * tq
    assert s_len % (2 * tq) == 0 and c_len % tq == 0 and (n_batch * s_len) % c_len == 0
    n_pair = s_len // (2 * tq)
    ctx_k0 = n_batch * s_len // c_len
    kv_bytes = 4 * (s_len + c_len) * hd * 2
    o_lat = pl.pallas_call(
        functools.partial(_attn_lat_kernel, tq, groups, hd, kc),
        grid=(n_batch, n_kv, n_pair),
        in_specs=[
            pl.BlockSpec((2 * tq, gw), lambda b, kv, i: (b * n_pair + i, kv)),
            pl.BlockSpec((2 * tq, gw), lambda b, kv, i: (b * n_pair + jnp.minimum(i + 1, n_pair - 1), kv)),
            pl.BlockSpec((s_len, hd), lambda b, kv, i: (b, n_heads + kv)),
            pl.BlockSpec((s_len, hd), lambda b, kv, i: (b, n_heads + n_kv + kv)),
            pl.BlockSpec((c_len, hd), lambda b, kv, i: (ctx_k0 + b, n_heads + kv)),
            pl.BlockSpec((c_len, hd), lambda b, kv, i: (ctx_k0 + b, n_heads + n_kv + kv)),
        ],
        out_specs=pl.BlockSpec((2 * tq, gw), lambda b, kv, i: (b * n_pair + i, kv)),
        out_shape=jax.ShapeDtypeStruct((n_batch * s_len, d_out), BF16),
        scratch_shapes=[pltpu.VMEM((m_rows, s_len + c_len), F32), pltpu.VMEM((m_rows, s_len + c_len), F32),
                        pltpu.VMEM((m_rows, LANES), F32), pltpu.VMEM((m_rows, LANES), F32)],
        compiler_params=_params(("arbitrary", "arbitrary", "arbitrary"),
                                kv_bytes + 12 * tq * gw * 2 + 2 * m_rows * (s_len + c_len) * 4
                                + 8 * m_rows * kc * 4),
        name="attention_lat",
    )(qkv, qkv, qkv, qkv, qkv, qkv)

    if not with_ctx:
        return o_lat, None
    nq_ctx = c_len // tq
    ctx_q0 = n_batch * s_len // tq
    o_ctx = pl.pallas_call(
        functools.partial(_attn_ctx_kernel, tq, groups, hd, kc),
        grid=(n_batch, n_kv, nq_ctx),
        in_specs=[
            pl.BlockSpec((tq, gw), lambda b, kv, i: (ctx_q0 + b * nq_ctx + i, kv)),
            pl.BlockSpec((c_len, hd), lambda b, kv, i: (ctx_k0 + b, n_heads + kv)),
            pl.BlockSpec((c_len, hd), lambda b, kv, i: (ctx_k0 + b, n_heads + n_kv + kv)),
        ],
        out_specs=pl.BlockSpec((tq, gw), lambda b, kv, i: (b * nq_ctx + i, kv)),
        out_shape=jax.ShapeDtypeStruct((n_batch * c_len, d_out), BF16),
        scratch_shapes=[pltpu.VMEM((m_rows, c_len), F32), pltpu.VMEM((m_rows, LANES), F32)],
        compiler_params=_params(("arbitrary", "arbitrary", "arbitrary"),
                                4 * c_len * hd * 2 + 4 * tq * gw * 2 + m_rows * c_len * 4
                                + 8 * m_rows * kc * 4),
        name="attention_ctx",
    )(qkv, qkv, qkv)
    return o_lat, o_ctx


def _moe_kernel(layer, tm, f, te_ref, tv_ref, nx_ref, nv_ref, src_ref, h_hbm, w1_hbm, w3_hbm, w2_hbm,
                y_ref, xbuf, st1, st3, st2, w13b, w2b, sem_x, sem_w):
    i = pl.program_id(0)
    slot = i % MOE_SLOTS
    valid = tv_ref[i] == 1

    def row_copy(t, r, sl):
        row = src_ref[t * tm + r]
        return pltpu.make_async_copy(h_hbm.at[pl.ds(row, 1)], xbuf.at[sl, pl.ds(r, 1)], sem_x.at[sl])

    def wait_rows(sl):
        pltpu.make_async_copy(h_hbm.at[pl.ds(0, tm)], xbuf.at[sl], sem_x.at[sl]).wait()

    def weight_copies(e):
        return (pltpu.make_async_copy(w1_hbm.at[layer, e], st1, sem_w.at[0]),
                pltpu.make_async_copy(w3_hbm.at[layer, e], st3, sem_w.at[1]),
                pltpu.make_async_copy(w2_hbm.at[layer, e], st2, sem_w.at[2]))

    @pl.when(jnp.logical_and(i == 0, valid))
    def _():
        for t in range(MOE_LOOKAHEAD):
            def body(r, _, t=t):
                row_copy(t, r, t).start()
                return 0
            lax.fori_loop(0, tm, body, 0, unroll=8)
        for cp in weight_copies(te_ref[0]):
            cp.start(priority=1)

    new_expert = jnp.logical_or(i == 0, te_ref[i] != te_ref[jnp.maximum(i - 1, 0)])

    @pl.when(jnp.logical_and(new_expert, valid))
    def _():
        for cp in weight_copies(te_ref[i]):
            cp.wait()
        w13b[:, :f] = st1[...].astype(BF16)
        w13b[:, f:] = st3[...].astype(BF16)
        w2b[...] = st2[...].astype(BF16)

        @pl.when(nx_ref[i] >= 0)
        def _():
            for cp in weight_copies(nx_ref[i]):
                cp.start(priority=1)

    @pl.when(valid)
    def _():
        wait_rows(slot)
        x = _unpack_bf16_pairs(xbuf[slot])
        h13 = jnp.dot(x, w13b[...], preferred_element_type=F32)
        h1 = h13[:, :f]
        act = ((h1 * _sigmoid(h1)) * h13[:, f:]).astype(BF16)
        nslot = (i + MOE_LOOKAHEAD) % MOE_SLOTS
        for r in range(tm):
            row_copy(i + MOE_LOOKAHEAD, r, nslot).start()
        y_ref[...] = _pack_bf16_pairs(jnp.dot(act, w2b[...], preferred_element_type=F32))

    @pl.when(jnp.logical_not(valid))
    def _():
        y_ref[...] = jnp.zeros_like(y_ref)

        @pl.when(i < nv_ref[0] + MOE_LOOKAHEAD)
        def _():
            wait_rows(slot)


def moe_experts(hp, tile_expert, tile_valid, next_expert, n_valid, src_rows, w1, w3, w2, layer, *, tm):
    half = hp.shape[1]
    d = 2 * half
    p = src_rows.shape[0]
    f = w2.shape[2]
    nt = p // tm
    any_spec = pl.BlockSpec(memory_space=pl.ANY)
    grid_spec = pltpu.PrefetchScalarGridSpec(
        num_scalar_prefetch=5,
        grid=(nt,),
        in_specs=[any_spec, any_spec, any_spec, any_spec],
        out_specs=pl.BlockSpec((tm, half), lambda i, te, tv, nx, nv, src: (i, 0)),
        scratch_shapes=[pltpu.VMEM((MOE_SLOTS, tm, half), jnp.uint32),
                        pltpu.VMEM((d, f), F32), pltpu.VMEM((d, f), F32), pltpu.VMEM((f, d), F32),
                        pltpu.VMEM((d, 2 * f), BF16), pltpu.VMEM((f, d), BF16),
                        pltpu.SemaphoreType.DMA((MOE_SLOTS,)), pltpu.SemaphoreType.DMA((3,))],
    )
    return pl.pallas_call(
        functools.partial(_moe_kernel, layer, tm, f),
        grid_spec=grid_spec,
        out_shape=jax.ShapeDtypeStruct((p, half), jnp.uint32),
        compiler_params=_params(("arbitrary",),
                                MOE_SLOTS * tm * half * 4 + 3 * d * f * 4 + 3 * d * f * 2 + 2 * tm * d * 4
                                + tm * d * 6 + tm * 2 * f * 8),
        name="moe_experts",
    )(tile_expert, tile_valid, next_expert, n_valid, src_rows, hp, w1, w3, w2)


def _combine_kernel(tm, mode, pos_ref, x_ref, g_ref, r_ref, y_hbm, *rest):
    if mode == "mid":
        gs_ref, sh_ref, o_ref, h_ref, ybuf, sem = rest
    else:
        gs_ref, o_ref, ybuf, sem = rest
    i = pl.program_id(0)
    nt = pl.num_programs(0)
    slot = i % 2

    def issue(t, sl):
        def body(r, _):
            for k in range(2):
                row = pos_ref[2 * (t * tm + r) + k]
                pltpu.make_async_copy(y_hbm.at[pl.ds(row, 1)], ybuf.at[sl, k, pl.ds(r, 1)],
                                      sem.at[sl]).start()
            return 0
        lax.fori_loop(0, tm, body, 0, unroll=4)

    @pl.when(i == 0)
    def _():
        issue(0, 0)

    @pl.when(i + 1 < nt)
    def _():
        issue(i + 1, 1 - slot)

    for k in range(2):
        pltpu.make_async_copy(y_hbm.at[pl.ds(0, tm)], ybuf.at[slot, k], sem.at[slot]).wait()
    rinfo = r_ref[...]
    moe = (rinfo[:, 2:3] * _unpack_bf16_pairs(ybuf[slot, 0], F32)
           + rinfo[:, 3:4] * _unpack_bf16_pairs(ybuf[slot, 1], F32))
    x_new = x_ref[...] + g_ref[...] * moe
    if mode == "mid":
        o_ref[...] = x_new
        h_ref[...] = _modulate(x_new, gs_ref[...], sh_ref[...]).astype(h_ref.dtype)
    else:
        ms = jnp.mean(x_new * x_new, axis=-1, keepdims=True)
        o_ref[...] = x_new * lax.rsqrt(ms + EPS) * gs_ref[...]


def moe_combine(x, gate, rinfo, y, pos, gs, sh, *, n_rows, s_len, n_batch, tm=128):
    d = x.shape[1]
    mode = "mid" if sh is not None else "last"
    assert n_rows % tm == 0 and n_rows <= x.shape[0]
    cls = _cls_of_tile(tm, s_len, s_len * n_batch, n_batch)
    row_spec = pl.BlockSpec((tm, d), lambda i, pos: (i, 0))
    cls_spec = pl.BlockSpec((None, 1, d), lambda i, pos: (cls(i), 0, 0))
    in_specs = [row_spec, cls_spec, pl.BlockSpec((tm, LANES), lambda i, pos: (i, 0)),
                pl.BlockSpec(memory_space=pl.ANY)]
    if mode == "mid":
        in_specs += [cls_spec, cls_spec]
        operands = (gs, sh)
        out_specs = [row_spec, row_spec]
        out_shape = [jax.ShapeDtypeStruct((n_rows, d), F32), jax.ShapeDtypeStruct((n_rows, d), BF16)]
    else:
        in_specs += [pl.BlockSpec((1, d), lambda i, pos: (0, 0))]
        operands = (gs.reshape(1, d),)
        out_specs = row_spec
        out_shape = jax.ShapeDtypeStruct((n_rows, d), F32)
    grid_spec = pltpu.PrefetchScalarGridSpec(
        num_scalar_prefetch=1,
        grid=(n_rows // tm,),
        in_specs=in_specs,
        out_specs=out_specs,
        scratch_shapes=[pltpu.VMEM((2, 2, tm, d // 2), jnp.uint32), pltpu.SemaphoreType.DMA((2,))],
    )
    return pl.pallas_call(
        functools.partial(_combine_kernel, tm, mode),
        grid_spec=grid_spec,
        out_shape=out_shape,
        compiler_params=_params(("arbitrary",), 4 * tm * d * 4 + 6 * tm * d * 4 + 4 * tm * d * 4),
        name="moe_combine_" + mode,
    )(pos, x, gate, rinfo, y, *operands)


def _route_plan(rinfo, n_experts, tm):
    r = rinfo.shape[0]
    e = rinfo[:, 0:2].astype(jnp.int32).reshape(-1)
    n_assign = 2 * r
    onehot = (e[:, None] == jnp.arange(n_experts, dtype=jnp.int32)[None, :]).astype(jnp.int32)
    csum = jnp.cumsum(onehot, axis=0)
    rank = jnp.sum(csum * onehot, axis=1) - 1
    counts = csum[-1]
    ntile = (counts + tm - 1) // tm
    tend = jnp.cumsum(ntile)
    tstart = tend - ntile
    pos = tstart[e] * tm + rank
    n_tiles = n_assign // tm + n_experts + MOE_LOOKAHEAD
    p = n_tiles * tm
    src = jnp.zeros((p,), jnp.int32).at[pos].set(jnp.arange(n_assign, dtype=jnp.int32) // 2)
    tid = jnp.arange(n_tiles, dtype=jnp.int32)
    tile_valid = (tid < tend[-1]).astype(jnp.int32)
    tile_expert = jnp.minimum(jnp.sum((tid[:, None] >= tend[None, :]).astype(jnp.int32), axis=1),
                              n_experts - 1)
    nxt = tend[tile_expert]
    next_expert = jnp.where(nxt < tend[-1], tile_expert[jnp.minimum(nxt, n_tiles - 1)], -1)
    last = jnp.maximum(tend[-1] - 1, 0)
    tile_expert = jnp.where(tile_valid == 1, tile_expert, tile_expert[last])
    return pos, src, tile_expert, tile_valid, next_expert.astype(jnp.int32), tend[-1:].astype(jnp.int32)


def _rope_tables(s_len, hd, tm):
    rows = s_len // ROPE_GRID_W
    t_row = jnp.repeat(jnp.arange(rows), ROPE_GRID_W).astype(F32)
    t_col = jnp.tile(jnp.arange(ROPE_GRID_W), rows).astype(F32)
    n_f = hd // 4
    inv = ROPE_THETA ** (-jnp.arange(n_f, dtype=F32) / n_f)
    ang = jnp.concatenate([t_row[:, None] * inv, t_col[:, None] * inv], axis=-1)
    cos, sin = jnp.cos(ang), jnp.sin(ang)
    cos2 = jnp.concatenate([cos, cos], axis=-1)
    sin2 = jnp.concatenate([-sin, sin], axis=-1)
    cos2 = jnp.concatenate([cos2, jnp.ones((tm, hd), F32)], axis=0)
    sin2 = jnp.concatenate([sin2, jnp.zeros((tm, hd), F32)], axis=0)
    return cos2, sin2


def kernel(x, c, ctx, c_ctx, ada_w, ada_b, norm_mix, norm_ffn, final_norm_g, lru_w_in, lru_conv_w,
           lru_conv_b, lru_w_a, lru_b_a, lru_w_i, lru_b_i, lru_lam, lru_w_out, attn_w_qkv, attn_q_norm,
           attn_k_norm, attn_w_o, moe_w_rg, moe_b_rg, moe_w_re, moe_b_re, moe_w1, moe_w3, moe_w2):
    n_batch, s_len, d = x.shape
    c_len = ctx.shape[1]
    depth = ada_w.shape[0]
    n_ada = ada_w.shape[2] // d
    d_rnn = lru_w_in.shape[2] // 2
    hd = attn_q_norm.shape[1]
    n_heads = d // hd
    n_kv = (attn_w_qkv.shape[2] // hd - n_heads) // 2
    n_groups = moe_w_rg.shape[2]
    n_experts = moe_w_re.shape[2]
    per_group = n_experts // n_groups
    n_lat = n_batch * s_len
    tm = 512 if (s_len % 512 == 0 and (n_batch * c_len) % 512 == 0) else 256
    moe_tm = 256
    geo = dict(s_len=s_len, n_batch=n_batch)
    assert n_batch + 1 <= NCLS_PAD and n_groups + n_experts <= LANES

    xs = jnp.concatenate([x.reshape(n_lat, d), ctx.reshape(n_batch * c_len, d)], axis=0)

    cond = jnp.zeros((NCLS_PAD, d), F32).at[:n_batch].set(c).at[n_batch].set(c_ctx)
    mods = ada_all(cond, ada_w, ada_b).reshape(depth, NCLS_PAD, n_ada, d)
    cos_tbl, sin_tbl = _rope_tables(s_len, hd, tm)

    def cls_vec(v):
        return v.reshape(NCLS_PAD, 1, d)

    n_mixers = 2
    n_all = xs.shape[0]

    def mix_mod(l):
        return cls_vec(norm_mix[l] * (1.0 + mods[l, :, 1])), cls_vec(mods[l, :, 0])

    h = modulate(xs, *mix_mod(0), **geo)
    for l in range(depth):
        last = l == depth - 1
        sh_m, sc_m, g_m, sh_f, sc_f, g_f = [mods[l, :, j] for j in range(n_ada)]
        j = l // n_mixers
        if l % n_mixers == 0:
            yg = matmul(h, lru_w_in, j, n_cols=d_rnn, col0=0, out_dtype=BF16, act="gelu", tm=tm)
            xr = matmul(h, lru_w_in, j, n_cols=d_rnn, col0=d_rnn, out_dtype=F32, tm=tm)
            z_lat, z_ctx = lru_mix(xr, yg, j, lru_conv_w, lru_conv_b, lru_w_a, lru_w_i, lru_b_a,
                                   lru_b_i, lru_lam, c_len=c_len, **geo)
            xs = matmul_resid(z_lat, lru_w_out, j, xs, cls_vec(g_m),
                              a_tail=None if last else z_ctx, tm=tm, **geo)
        else:
            q_scale = hd ** -0.5 * math.log2(math.e)
            gain = jnp.concatenate([jnp.tile(attn_q_norm[j] * q_scale, n_heads),
                                    jnp.tile(attn_k_norm[j], n_kv),
                                    jnp.ones((n_kv * hd,), F32)]).reshape(1, -1)
            qkv = matmul_qkv(h, attn_w_qkv, j, gain, cos_tbl, sin_tbl,
                             n_norm_cols=(n_heads + n_kv) * hd, hd=hd, tm=tm, **geo)
            o_lat, o_ctx = attention(qkv, n_heads=n_heads, n_kv=n_kv, hd=hd, c_len=c_len, d_out=d,
                                     with_ctx=not last, **geo)
            xs = matmul_resid(o_lat, attn_w_o, j, xs, cls_vec(g_m), a_tail=o_ctx, tm=tm, **geo)

        n_rows = n_lat if last else n_all
        wr = jnp.zeros((d, LANES), F32).at[:, :n_groups].set(moe_w_rg[l])
        wr = wr.at[:, n_groups:n_groups + n_experts].set(moe_w_re[l])
        br = jnp.zeros((1, LANES), F32).at[0, :n_groups].set(moe_b_rg[l])
        br = br.at[0, n_groups:n_groups + n_experts].set(moe_b_re[l])
        wr_hi = wr.astype(BF16)
        wr = jnp.concatenate([wr_hi, (wr - wr_hi.astype(F32)).astype(BF16)], axis=1)
        hp, rinfo = modulate_router(xs, cls_vec(norm_ffn[l] * (1.0 + sc_f)), cls_vec(sh_f), wr, br,
                                    n_rows=n_rows, n_groups=n_groups, per_group=per_group, **geo)
        pos, src, tile_expert, tile_valid, next_expert, n_valid = _route_plan(rinfo, n_experts, moe_tm)
        y = moe_experts(hp, tile_expert, tile_valid, next_expert, n_valid, src, moe_w1, moe_w3, moe_w2,
                        l, tm=moe_tm)
        if last:
            out = moe_combine(xs, cls_vec(g_f), rinfo, y, pos, final_norm_g, None, n_rows=n_rows, **geo)
        else:
            xs, h = moe_combine(xs, cls_vec(g_f), rinfo, y, pos, *mix_mod(l + 1), n_rows=n_rows, **geo)

    return out.reshape(n_batch, s_len, d)
```

```python
import functools
import math

import jax
import jax.numpy as jnp
from jax import lax
from jax.experimental import pallas as pl
from jax.experimental.pallas import tpu as pltpu

EPS = 1e-6
LRU_C = 8.0
ROPE_THETA = 10000.0
ROPE_GRID_W = 64
LANES = 128
SUBLANES = 8
NCLS_PAD = 8
V7X_VMEM_CAP = 56 * 1024 * 1024
MOE_LOOKAHEAD = 2
MOE_SLOTS = MOE_LOOKAHEAD + 1
BF16 = jnp.bfloat16
F32 = jnp.float32


def _vmem_limit(nbytes):
    return int(min(V7X_VMEM_CAP, max(16 * 1024 * 1024, nbytes * 5 // 4 + (4 << 20))))


def _params(sem, nbytes):
    return pltpu.CompilerParams(dimension_semantics=sem, vmem_limit_bytes=_vmem_limit(nbytes))


def _sigmoid(x):
    return 1.0 / (1.0 + jnp.exp(-x))


def _gelu_tanh(x):
    return 0.5 * x * (1.0 + jnp.tanh(math.sqrt(2.0 / math.pi) * (x + 0.044715 * (x * x * x))))


def _ada_kernel(c_ref, w_ref, b_ref, o_ref):
    c = c_ref[...]
    s = (c * _sigmoid(c)).astype(BF16)
    w = w_ref[...].astype(BF16)
    o_ref[...] = jnp.dot(s, w, preferred_element_type=F32) + b_ref[...]


def ada_all(cond, ada_w, ada_b):
    depth, d, n = ada_w.shape
    tn = 512
    return pl.pallas_call(
        _ada_kernel,
        grid=(depth, n // tn),
        in_specs=[
            pl.BlockSpec((NCLS_PAD, d), lambda l, j: (0, 0)),
            pl.BlockSpec((None, d, tn), lambda l, j: (l, 0, j)),
            pl.BlockSpec((None, 1, tn), lambda l, j: (l, 0, j)),
        ],
        out_specs=pl.BlockSpec((None, NCLS_PAD, tn), lambda l, j: (l, 0, j)),
        out_shape=jax.ShapeDtypeStruct((depth, NCLS_PAD, n), F32),
        compiler_params=_params(("arbitrary", "arbitrary"), 2 * d * tn * 4 + d * tn * 2),
        name="ada_all",
    )(cond, ada_w, ada_b.reshape(depth, 1, n))


def _modulate(x, gs, sh):
    ms = jnp.mean(x * x, axis=-1, keepdims=True)
    return x * lax.rsqrt(ms + EPS) * gs + sh


def _pack_bf16_pairs(h):
    half = h.shape[1] // 2
    bits = lax.bitcast_convert_type(h.astype(BF16).astype(F32), jnp.uint32)
    return (bits[:, :half] >> 16) | (bits[:, half:] & jnp.uint32(0xFFFF0000))


def _unpack_bf16_pairs(words, dtype=BF16):
    lo = lax.bitcast_convert_type(words << 16, F32).astype(dtype)
    hi = lax.bitcast_convert_type(words & jnp.uint32(0xFFFF0000), F32).astype(dtype)
    return jnp.concatenate([lo, hi], axis=1)


def _mod_router_kernel(n_groups, per_group, x_ref, gs_ref, sh_ref, wr_ref, br_ref, h_ref, r_ref):
    h = _modulate(x_ref[...], gs_ref[...], sh_ref[...])
    h_ref[...] = _pack_bf16_pairs(h)
    hh = h.astype(BF16)
    hl = (h - hh.astype(F32)).astype(BF16)
    r_hi = jnp.dot(hh, wr_ref[...], preferred_element_type=F32)
    r_lo = jnp.dot(hl, wr_ref[:, :LANES], preferred_element_type=F32)
    logits = r_hi[:, :LANES] + r_hi[:, LANES:] + r_lo + br_ref[...]
    lane = lax.broadcasted_iota(jnp.int32, logits.shape, 1)
    neg = jnp.float32(-jnp.inf)
    big = jnp.int32(1 << 20)
    gl = jnp.where(lane < n_groups, logits, neg)
    gmax = jnp.max(gl, axis=-1, keepdims=True)
    gsum = jnp.sum(jnp.exp(gl - gmax), axis=-1, keepdims=True)
    g_top = 1.0 / gsum
    g_idx = jnp.min(jnp.where(gl == gmax, lane, big), axis=-1, keepdims=True)
    lo = n_groups + per_group * g_idx
    el = jnp.where((lane >= lo) & (lane < lo + per_group), logits, neg)
    m1 = jnp.max(el, axis=-1, keepdims=True)
    i1 = jnp.min(jnp.where(el == m1, lane, big), axis=-1, keepdims=True)
    el2 = jnp.where(lane == i1, neg, el)
    m2 = jnp.max(el2, axis=-1, keepdims=True)
    i2 = jnp.min(jnp.where(el2 == m2, lane, big), axis=-1, keepdims=True)
    t = jnp.exp(m2 - m1)
    w1 = g_top / (1.0 + t)
    w2 = g_top * t / (1.0 + t)
    e1 = (i1 - n_groups).astype(F32)
    e2 = (i2 - n_groups).astype(F32)
    r_ref[...] = jnp.where(lane == 0, e1, jnp.where(lane == 1, e2,
                           jnp.where(lane == 2, w1, jnp.where(lane == 3, w2, 0.0))))


def _mod_first_kernel(n_lat_tiles, xl_ref, xc_ref, gs_ref, sh_ref, xs_ref, h_ref):
    def emit(src_ref):
        x = src_ref[...]
        xs_ref[...] = x
        h_ref[...] = _modulate(x, gs_ref[...], sh_ref[...]).astype(h_ref.dtype)

    @pl.when(pl.program_id(0) < n_lat_tiles)
    def _():
        emit(xl_ref)

    @pl.when(pl.program_id(0) >= n_lat_tiles)
    def _():
        emit(xc_ref)


def _cls_of_tile(tm, s_len, n_lat, n_batch):
    def f(i):
        r0 = i * tm
        return jnp.where(r0 < n_lat, r0 // s_len, n_batch)
    return f


def modulate_first(x_lat, x_ctx, gs, sh, *, s_len, n_batch, tm=256):
    n_lat, d = x_lat.shape
    n_ctx = x_ctx.shape[0]
    assert n_lat % tm == 0 and n_ctx % tm == 0
    lt, ct = n_lat // tm, n_ctx // tm
    cls = _cls_of_tile(tm, s_len, n_lat, n_batch)
    row_spec = pl.BlockSpec((tm, d), lambda i: (i, 0))
    return pl.pallas_call(
        functools.partial(_mod_first_kernel, lt),
        grid=(lt + ct,),
        in_specs=[
            pl.BlockSpec((tm, d), lambda i: (jnp.minimum(i, lt - 1), 0)),
            pl.BlockSpec((tm, d), lambda i: (jnp.clip(i - lt, 0, ct - 1), 0)),
            pl.BlockSpec((None, 1, d), lambda i: (cls(i), 0, 0)),
            pl.BlockSpec((None, 1, d), lambda i: (cls(i), 0, 0)),
        ],
        out_specs=[row_spec, row_spec],
        out_shape=[jax.ShapeDtypeStruct((n_lat + n_ctx, d), F32),
                   jax.ShapeDtypeStruct((n_lat + n_ctx, d), BF16)],
        compiler_params=_params(("arbitrary",), 2 * tm * d * (4 + 4 + 4 + 2)),
        name="modulate_first",
    )(x_lat, x_ctx, gs, sh)


def modulate_router(x, gs, sh, wr, br, *, n_rows, n_groups, per_group, s_len, n_batch, tm=256):
    r, d = n_rows, x.shape[1]
    assert r % tm == 0 and r <= x.shape[0]
    cls = _cls_of_tile(tm, s_len, s_len * n_batch, n_batch)
    return pl.pallas_call(
        functools.partial(_mod_router_kernel, n_groups, per_group),
        grid=(r // tm,),
        in_specs=[
            pl.BlockSpec((tm, d), lambda i: (i, 0)),
            pl.BlockSpec((None, 1, d), lambda i: (cls(i), 0, 0)),
            pl.BlockSpec((None, 1, d), lambda i: (cls(i), 0, 0)),
            pl.BlockSpec((d, 2 * LANES), lambda i: (0, 0)),
            pl.BlockSpec((1, LANES), lambda i: (0, 0)),
        ],
        out_specs=[
            pl.BlockSpec((tm, d // 2), lambda i: (i, 0)),
            pl.BlockSpec((tm, LANES), lambda i: (i, 0)),
        ],
        out_shape=[jax.ShapeDtypeStruct((r, d // 2), jnp.uint32),
                   jax.ShapeDtypeStruct((r, LANES), F32)],
        compiler_params=_params(("arbitrary",), 2 * tm * d * 8 + 2 * d * LANES * 4 + 4 * tm * d * 4),
        name="modulate_router",
    )(x, gs, sh, wr, br)


def _cast_weight_once(wsel, w_hbm, wbf_ref, stage_ref, sem):
    layer, col0, tn = wsel
    j = pl.program_id(0)

    def copy(jj):
        start = col0 + jj * tn
        if not isinstance(start, int):
            start = pl.multiple_of(start, tn)
        return pltpu.make_async_copy(w_hbm.at[layer, :, pl.ds(start, tn)], stage_ref, sem.at[0])

    @pl.when(pl.program_id(1) == 0)
    def _():
        @pl.when(j == 0)
        def _():
            copy(0).start()

        copy(j).wait()
        wbf_ref[...] = stage_ref[...].astype(BF16)

        @pl.when(j + 1 < pl.num_programs(0))
        def _():
            copy(j + 1).start()


def _mm_plain_kernel(wsel, act, a_ref, w_hbm, o_ref, wbf_ref, stage_ref, sem):
    _cast_weight_once(wsel, w_hbm, wbf_ref, stage_ref, sem)
    acc = jnp.dot(a_ref[...], wbf_ref[...], preferred_element_type=F32)
    if act == "gelu":
        acc = _gelu_tanh(acc)
    o_ref[...] = acc.astype(o_ref.dtype)


def _mm_resid_kernel(wsel, n_main_tiles, a_ref, a2_ref, w_hbm, x_ref, g_ref, o_ref, wbf_ref, stage_ref,
                     sem):
    _cast_weight_once(wsel, w_hbm, wbf_ref, stage_ref, sem)
    i = pl.program_id(1)

    def emit(src_ref):
        acc = jnp.dot(src_ref[...], wbf_ref[...], preferred_element_type=F32)
        o_ref[...] = x_ref[...] + g_ref[...] * acc

    @pl.when(i < n_main_tiles)
    def _():
        emit(a_ref)

    @pl.when(i >= n_main_tiles)
    def _():
        emit(a2_ref)


def _mm_qkv_kernel(wsel, n_norm_tiles, hd, a_ref, w_hbm, gain_ref, cos_ref, sin_ref, o_ref, wbf_ref,
                   stage_ref, sem):
    _cast_weight_once(wsel, w_hbm, wbf_ref, stage_ref, sem)
    acc = jnp.dot(a_ref[...], wbf_ref[...], preferred_element_type=F32)
    j = pl.program_id(0)

    @pl.when(j < n_norm_tiles)
    def _():
        cos = cos_ref[...]
        sin = sin_ref[...]
        for h in range(acc.shape[1] // hd):
            xh = acc[:, h * hd:(h + 1) * hd]
            ms = jnp.mean(xh * xh, axis=-1, keepdims=True)
            y = xh * lax.rsqrt(ms + EPS) * gain_ref[:, h * hd:(h + 1) * hd]
            y = y * cos + pltpu.roll(y, hd // 2, 1) * sin
            o_ref[:, h * hd:(h + 1) * hd] = y.astype(o_ref.dtype)

    @pl.when(j >= n_norm_tiles)
    def _():
        o_ref[...] = acc.astype(o_ref.dtype)


_W_HBM = pl.BlockSpec(memory_space=pl.ANY)


def _w_scratch(k, tn):
    return [pltpu.VMEM((k, tn), BF16), pltpu.VMEM((k, tn), F32), pltpu.SemaphoreType.DMA((1,))]


def _mm_vmem(tm, k, tn, out_bytes, extra=0):
    return (2 * (tm * k * 2 + tm * tn * out_bytes) + k * tn * 4 + k * tn * 2 + 2 * tm * tn * 4
            + extra)


def matmul(a, w, layer, *, n_cols, col0=0, out_dtype, act=None, tm=512, tn=1024):
    r, k = a.shape
    tn = min(tn, n_cols)
    assert r % tm == 0 and n_cols % tn == 0 and col0 % tn == 0
    ob = jnp.dtype(out_dtype).itemsize
    return pl.pallas_call(
        functools.partial(_mm_plain_kernel, (layer, col0, tn), act),
        grid=(n_cols // tn, r // tm),
        in_specs=[pl.BlockSpec((tm, k), lambda j, i: (i, 0)), _W_HBM],
        out_specs=pl.BlockSpec((tm, tn), lambda j, i: (i, j)),
        out_shape=jax.ShapeDtypeStruct((r, n_cols), out_dtype),
        scratch_shapes=_w_scratch(k, tn),
        compiler_params=_params(("arbitrary", "arbitrary"), _mm_vmem(tm, k, tn, ob)),
        name="matmul_" + (act or "plain"),
    )(a, w)


def matmul_resid(a, w, layer, xres, gate, *, s_len, n_batch, a_tail=None, tm=512, tn=1024):
    n = xres.shape[1]
    k = a.shape[1]
    tn = min(tn, n)
    n_main = a.shape[0] // tm
    if a_tail is None:
        a_tail, n_tail, r = a, 1, a.shape[0]
    else:
        n_tail, r = a_tail.shape[0] // tm, a.shape[0] + a_tail.shape[0]
    assert r % tm == 0 and n % tn == 0 and a.shape[0] % tm == 0 and r <= xres.shape[0]
    cls = _cls_of_tile(tm, s_len, s_len * n_batch, n_batch)
    return pl.pallas_call(
        functools.partial(_mm_resid_kernel, (layer, 0, tn), n_main),
        grid=(n // tn, r // tm),
        in_specs=[pl.BlockSpec((tm, k), lambda j, i: (jnp.minimum(i, n_main - 1), 0)),
                  pl.BlockSpec((tm, k), lambda j, i: (jnp.clip(i - n_main, 0, n_tail - 1), 0),
                               pipeline_mode=pl.Buffered(1)),
                  _W_HBM,
                  pl.BlockSpec((tm, tn), lambda j, i: (i, j)),
                  pl.BlockSpec((None, 1, tn), lambda j, i: (cls(i), 0, j))],
        out_specs=pl.BlockSpec((tm, tn), lambda j, i: (i, j)),
        out_shape=jax.ShapeDtypeStruct((r, n), F32),
        scratch_shapes=_w_scratch(k, tn),
        compiler_params=_params(("arbitrary", "arbitrary"),
                                _mm_vmem(tm, k, tn, 4, tm * k * 2 + 2 * tm * tn * 4)),
        name="matmul_resid",
    )(a, a_tail, w, xres, gate)


def matmul_qkv(a, w, layer, gain, cos_tbl, sin_tbl, *, n_norm_cols, hd, s_len, n_batch, tm=512, tn=1024):
    r, k = a.shape
    n = w.shape[2]
    while n_norm_cols % tn or n % tn:
        tn //= 2
    assert tn % hd == 0 and s_len % tm == 0
    n_lat = s_len * n_batch
    pos_blocks = s_len // tm

    def pos(i):
        return jnp.where(i * tm < n_lat, i % pos_blocks, pos_blocks)

    return pl.pallas_call(
        functools.partial(_mm_qkv_kernel, (layer, 0, tn), n_norm_cols // tn, hd),
        grid=(n // tn, r // tm),
        in_specs=[pl.BlockSpec((tm, k), lambda j, i: (i, 0)),
                  _W_HBM,
                  pl.BlockSpec((1, tn), lambda j, i: (0, j)),
                  pl.BlockSpec((tm, hd), lambda j, i: (pos(i), 0)),
                  pl.BlockSpec((tm, hd), lambda j, i: (pos(i), 0))],
        out_specs=pl.BlockSpec((tm, tn), lambda j, i: (i, j)),
        out_shape=jax.ShapeDtypeStruct((r, n), BF16),
        scratch_shapes=_w_scratch(k, tn),
        compiler_params=_params(("arbitrary", "arbitrary"),
                                _mm_vmem(tm, k, tn, 2, 4 * tm * hd * 4 + tm * tn * 4)),
        name="matmul_qkv",
    )(a, w, gain, cos_tbl, sin_tbl)


def _seg_geometry(t_len):
    ls = t_len // SUBLANES
    assert ls % SUBLANES == 0
    return ls, ls + SUBLANES


def _lru_kernel(tc, xl_ref, xc_ref, yl_ref, yc_ref, cw_ref, cb_ref, wa_ref, wi_ref, ba_ref, bi_ref,
                lam_ref, ol_ref, oc_ref, cvl_ref, cvc_ref, accl_ref, accc_ref,
                al_ref, ul_ref, ac_ref, uc_ref):
    w = xl_ref.shape[1]
    cw = cw_ref[...]
    cb = cb_ref[...]

    def conv_seq(x_ref, cv_ref):
        t_len = x_ref.shape[0]
        nchunk = t_len // tc

        def body(ci, _):
            t0 = pl.multiple_of(ci * tc, tc)
            main = x_ref[pl.ds(t0, tc), :]
            p0 = pl.multiple_of(jnp.maximum(t0 - 8, 0), 8)
            n0 = pl.multiple_of(jnp.minimum(t0 + tc, t_len - 8), 8)
            prev8 = jnp.where(ci > 0, x_ref[pl.ds(p0, 8), :], 0.0)
            next8 = jnp.where(ci < nchunk - 1, x_ref[pl.ds(n0, 8), :], 0.0)
            ext = jnp.concatenate([prev8, main, next8], axis=0)
            ne = tc + 16
            xm1 = pltpu.roll(ext, 1, 0)[8:8 + tc]
            xp1 = pltpu.roll(ext, ne - 1, 0)[8:8 + tc]
            xp2 = pltpu.roll(ext, ne - 2, 0)[8:8 + tc]
            cv_ref[pl.ds(t0, tc), :] = (cw[0:1] * xm1 + cw[1:2] * main + cw[2:3] * xp1
                                        + cw[3:4] * xp2 + cb)
            return 0

        lax.fori_loop(0, nchunk, body, 0)

    conv_seq(xl_ref, cvl_ref)
    conv_seq(xc_ref, cvc_ref)

    def run_seq(d, cv_ref, acc_ref, a_ref, u_ref, carry_in):
        reverse = d == 1
        t_len = cv_ref.shape[0]
        ls, stride = _seg_geometry(t_len)
        piece = min(tc, ls)
        n_lt = w // LANES
        wa, wi, ba, bi, lam = wa_ref[d], wi_ref[d], ba_ref[d], bi_ref[d], lam_ref[d]
        nsp = -LRU_C * (jnp.maximum(-lam, 0.0) + jnp.log(1.0 + jnp.exp(-jnp.abs(lam))))
        c0 = (0.5 * math.log2(math.e)) * nsp

        def seg_row(t):
            return pl.multiple_of(t + SUBLANES * (t // ls), SUBLANES)

        def coeff_body(ci, _):
            t0 = pl.multiple_of(ci * tc, tc)
            x = cv_ref[pl.ds(t0, tc), :]
            xb = x.astype(BF16)
            tr = jnp.tanh(jnp.dot(xb, wa, preferred_element_type=F32) + ba)
            ti = jnp.tanh(jnp.dot(xb, wi, preferred_element_type=F32) + bi)
            a = jnp.exp2(c0 + c0 * tr)
            u = jnp.sqrt(1.0 - a * a) * ((0.5 + 0.5 * ti) * x)
            for p in range(tc // piece):
                row = seg_row(t0 + p * piece)
                for lt in range(n_lt):
                    lanes = slice(lt * LANES, (lt + 1) * LANES)
                    a_ref[lt, pl.ds(row, piece), :] = a[p * piece:(p + 1) * piece, lanes]
                    u_ref[lt, pl.ds(row, piece), :] = u[p * piece:(p + 1) * piece, lanes]
            return 0

        lax.fori_loop(0, t_len // tc, coeff_body, 0, unroll=2 if (t_len // tc) % 2 == 0 else 1)

        def scan_body(k, carry):
            r = (ls - 1 - k) if reverse else k
            idx = pl.ds(r, SUBLANES, stride=stride)
            out = []
            for lt in range(n_lt):
                h, p = carry[lt]
                a = a_ref[lt, idx, :]
                h = a * h + u_ref[lt, idx, :]
                p = a * p
                u_ref[lt, idx, :] = h
                a_ref[lt, idx, :] = p
                out.append((h, p))
            return tuple(out)

        init = tuple((jnp.zeros((SUBLANES, LANES), F32), jnp.ones((SUBLANES, LANES), F32))
                     for _ in range(n_lt))
        ends = lax.fori_loop(0, ls, scan_body, init, unroll=8)
        h_end = jnp.concatenate([e[0] for e in ends], axis=1)
        p_end = jnp.concatenate([e[1] for e in ends], axis=1)

        c = carry_in
        seg_in = [None] * SUBLANES
        for s in (range(SUBLANES - 1, -1, -1) if reverse else range(SUBLANES)):
            seg_in[s] = c
            c = p_end[s:s + 1] * c + h_end[s:s + 1]
        for s in range(SUBLANES):
            for q in range(ls // piece):
                src = s * stride + q * piece
                dst = s * ls + q * piece
                for lt in range(n_lt):
                    lanes = slice(lt * LANES, (lt + 1) * LANES)
                    h = (u_ref[lt, src:src + piece, :]
                         + a_ref[lt, src:src + piece, :] * seg_in[s][:, lanes])
                    if d == 0:
                        acc_ref[dst:dst + piece, lanes] = h
                    else:
                        acc_ref[dst:dst + piece, lanes] = acc_ref[dst:dst + piece, lanes] + h
        return c

    for d in range(2):
        carry = run_seq(d, cvc_ref, accc_ref, ac_ref, uc_ref, jnp.zeros((1, w), F32))
        run_seq(d, cvl_ref, accl_ref, al_ref, ul_ref, carry)

    def finish(acc_ref, y_ref, o_ref):
        nchunk = acc_ref.shape[0] // tc

        def body(ci, _):
            t0 = pl.multiple_of(ci * tc, tc)
            o_ref[pl.ds(t0, tc), :] = (acc_ref[pl.ds(t0, tc), :]
                                       * y_ref[pl.ds(t0, tc), :].astype(F32)).astype(o_ref.dtype)
            return 0

        lax.fori_loop(0, nchunk, body, 0)

    finish(accl_ref, yl_ref, ol_ref)
    finish(accc_ref, yc_ref, oc_ref)


def lru_mix(xr, yg, layer, conv_w, conv_b, w_a, w_i, b_a, b_i, lam, *, s_len, c_len, n_batch, tc=256):
    r, rn = xr.shape
    nblk, bw = w_a.shape[2], w_a.shape[3]
    assert s_len % tc == 0 and c_len % tc == 0 and (n_batch * s_len) % c_len == 0
    ctx0 = n_batch * s_len // c_len
    lat = lambda b, k: (b, k)
    ctx = lambda b, k: (ctx0 + b, k)
    vec = lambda b, k: (layer, 0, 0, k)
    seg_rows = lambda t: SUBLANES * _seg_geometry(t)[1]
    seg_buf = lambda t: pltpu.VMEM((bw // LANES, seg_rows(t), LANES), F32)
    n_dir = w_a.shape[1]
    return pl.pallas_call(
        functools.partial(_lru_kernel, tc),
        grid=(n_batch, nblk),
        in_specs=[
            pl.BlockSpec((s_len, bw), lat), pl.BlockSpec((c_len, bw), ctx),
            pl.BlockSpec((s_len, bw), lat), pl.BlockSpec((c_len, bw), ctx),
            pl.BlockSpec((None, conv_w.shape[1], bw), lambda b, k: (layer, 0, k)),
            pl.BlockSpec((None, 1, bw), lambda b, k: (layer, 0, k)),
            pl.BlockSpec((None, n_dir, None, bw, bw), lambda b, k: (layer, 0, k, 0, 0)),
            pl.BlockSpec((None, n_dir, None, bw, bw), lambda b, k: (layer, 0, k, 0, 0)),
            pl.BlockSpec((None, n_dir, 1, bw), vec), pl.BlockSpec((None, n_dir, 1, bw), vec),
            pl.BlockSpec((None, n_dir, 1, bw), vec),
        ],
        out_specs=[pl.BlockSpec((s_len, bw), lambda b, k: (b, k)),
                   pl.BlockSpec((c_len, bw), lambda b, k: (b, k))],
        out_shape=[jax.ShapeDtypeStruct((n_batch * s_len, rn), BF16),
                   jax.ShapeDtypeStruct((n_batch * c_len, rn), BF16)],
        scratch_shapes=[pltpu.VMEM((s_len, bw), F32), pltpu.VMEM((c_len, bw), F32),
                        pltpu.VMEM((s_len, bw), F32), pltpu.VMEM((c_len, bw), F32),
                        seg_buf(s_len), seg_buf(s_len), seg_buf(c_len), seg_buf(c_len)],
        compiler_params=_params(("arbitrary", "arbitrary"),
                                (s_len + c_len) * bw * (2 * 4 + 2 * 2 + 2 * 2 + 8)
                                + 2 * (seg_rows(s_len) + seg_rows(c_len)) * bw * 4
                                + 8 * bw * bw + 24 * tc * bw * 4),
        name="lru_mix",
    )(xr, xr, yg, yg, conv_w, conv_b.reshape(conv_b.shape[0], 1, rn),
      (0.5 * w_a).astype(BF16), (0.5 * w_i).astype(BF16),
      (0.5 * b_a).reshape(-1, n_dir, 1, rn), (0.5 * b_i).reshape(-1, n_dir, 1, rn),
      lam.reshape(-1, n_dir, 1, rn))


_DN_NT = (((1,), (1,)), ((), ()))


def _split_heads(q, groups, hd):
    return jnp.concatenate([q[:, g * hd:(g + 1) * hd] for g in range(groups)], axis=0)


def _attn_pass1(q4, k_refs, s_ref, m_ref, kc):
    m_rows = q4.shape[0]
    mx = jnp.full((m_rows, LANES), -jnp.inf, F32)
    off = 0
    for k_ref in k_refs:
        for c0 in range(0, k_ref.shape[0], kc):
            w = min(kc, k_ref.shape[0] - c0)
            s = lax.dot_general(q4, k_ref[c0:c0 + w, :], _DN_NT, preferred_element_type=F32)
            s_ref[:, off:off + w] = s
            for t in range(w // LANES):
                mx = jnp.maximum(mx, s[:, t * LANES:(t + 1) * LANES])
            off += w
    m_ref[...] = jnp.broadcast_to(jnp.max(mx, axis=-1, keepdims=True), (m_rows, LANES))


def _attn_pass2(v_refs, s_ref, m_ref, kc):
    m_rows = s_ref.shape[0]
    hd = v_refs[0].shape[1]
    mrow = m_ref[...]
    lsum = jnp.zeros((m_rows, LANES), F32)
    o = jnp.zeros((m_rows, hd), F32)
    off = 0
    for v_ref in v_refs:
        for c0 in range(0, v_ref.shape[0], kc):
            w = min(kc, v_ref.shape[0] - c0)
            ps = []
            for t in range(w // LANES):
                p = jnp.exp2(s_ref[:, off + t * LANES:off + (t + 1) * LANES] - mrow)
                lsum = lsum + p
                ps.append(p.astype(BF16))
            p_all = jnp.concatenate(ps, axis=1) if len(ps) > 1 else ps[0]
            o = o + jnp.dot(p_all, v_ref[c0:c0 + w, :], preferred_element_type=F32)
            off += w
    return o / jnp.sum(lsum, axis=-1, keepdims=True)


def _emit_heads(o, o_ref, row0, tq, groups, hd):
    for g in range(groups):
        o_ref[row0:row0 + tq, g * hd:(g + 1) * hd] = o[g * tq:(g + 1) * tq].astype(o_ref.dtype)


def _attn_lat_kernel(tq, groups, hd, kc, qc_ref, qn_ref, kl_ref, vl_ref, kcx_ref, vcx_ref, o_ref,
                     s0_ref, s1_ref, m0_ref, m1_ref):
    keys = [kcx_ref, kl_ref]
    vals = [vcx_ref, vl_ref]

    @pl.when(pl.program_id(2) == 0)
    def _():
        _attn_pass1(_split_heads(qc_ref[0:tq, :], groups, hd), keys, s0_ref, m0_ref, kc)

    _attn_pass1(_split_heads(qc_ref[tq:2 * tq, :], groups, hd), keys, s1_ref, m1_ref, kc)
    _emit_heads(_attn_pass2(vals, s0_ref, m0_ref, kc), o_ref, 0, tq, groups, hd)
    _attn_pass1(_split_heads(qn_ref[0:tq, :], groups, hd), keys, s0_ref, m0_ref, kc)
    _emit_heads(_attn_pass2(vals, s1_ref, m1_ref, kc), o_ref, tq, tq, groups, hd)


def _attn_ctx_kernel(tq, groups, hd, kc, q_ref, k_ref, v_ref, o_ref, s_ref, m_ref):
    _attn_pass1(_split_heads(q_ref[...], groups, hd), [k_ref], s_ref, m_ref, kc)
    _emit_heads(_attn_pass2([v_ref], s_ref, m_ref, kc), o_ref, 0, tq, groups, hd)


def attention(qkv, *, n_heads, n_kv, hd, s_len, c_len, n_batch, d_out, with_ctx=True, tq=128, kc=512):
    groups = n_heads // n_kv
    gw = groups * hd
    m_rows = groups * tq
    assert s_len % (2 * tq) == 0 and c_len % tq == 0 and (n_batch * s_len) % c_len == 0
    n_pair = s_len // (2 * tq)
    ctx_k0 = n_batch * s_len // c_len
    kv_bytes = 4 * (s_len + c_len) * hd * 2
    o_lat = pl.pallas_call(
        functools.partial(_attn_lat_kernel, tq, groups, hd, kc),
        grid=(n_batch, n_kv, n_pair),
        in_specs=[
            pl.BlockSpec((2 * tq, gw), lambda b, kv, i: (b * n_pair + i, kv)),
            pl.BlockSpec((2 * tq, gw), lambda b, kv, i: (b * n_pair + jnp.minimum(i + 1, n_pair - 1), kv)),
            pl.BlockSpec((s_len, hd), lambda b, kv, i: (b, n_heads + kv)),
            pl.BlockSpec((s_len, hd), lambda b, kv, i: (b, n_heads + n_kv + kv)),
            pl.BlockSpec((c_len, hd), lambda b, kv, i: (ctx_k0 + b, n_heads + kv)),
            pl.BlockSpec((c_len, hd), lambda b, kv, i: (ctx_k0 + b, n_heads + n_kv + kv)),
        ],
        out_specs=pl.BlockSpec((2 * tq, gw), lambda b, kv, i: (b * n_pair + i, kv)),
        out_shape=jax.ShapeDtypeStruct((n_batch * s_len, d_out), BF16),
        scratch_shapes=[pltpu.VMEM((m_rows, s_len + c_len), F32), pltpu.VMEM((m_rows, s_len + c_len), F32),
                        pltpu.VMEM((m_rows, LANES), F32), pltpu.VMEM((m_rows, LANES), F32)],
        compiler_params=_params(("arbitrary", "arbitrary", "arbitrary"),
                                kv_bytes + 12 * tq * gw * 2 + 2 * m_rows * (s_len + c_len) * 4
                                + 8 * m_rows * kc * 4),
        name="attention_lat",
    )(qkv, qkv, qkv, qkv, qkv, qkv)

    if not with_ctx:
        return o_lat, None
    nq_ctx = c_len // tq
    ctx_q0 = n_batch * s_len // tq
    o_ctx = pl.pallas_call(
        functools.partial(_attn_ctx_kernel, tq, groups, hd, kc),
        grid=(n_batch, n_kv, nq_ctx),
        in_specs=[
            pl.BlockSpec((tq, gw), lambda b, kv, i: (ctx_q0 + b * nq_ctx + i, kv)),
            pl.BlockSpec((c_len, hd), lambda b, kv, i: (ctx_k0 + b, n_heads + kv)),
            pl.BlockSpec((c_len, hd), lambda b, kv, i: (ctx_k0 + b, n_heads + n_kv + kv)),
        ],
        out_specs=pl.BlockSpec((tq, gw), lambda b, kv, i: (b * nq_ctx + i, kv)),
        out_shape=jax.ShapeDtypeStruct((n_batch * c_len, d_out), BF16),
        scratch_shapes=[pltpu.VMEM((m_rows, c_len), F32), pltpu.VMEM((m_rows, LANES), F32)],
        compiler_params=_params(("arbitrary", "arbitrary", "arbitrary"),
                                4 * c_len * hd * 2 + 4 * tq * gw * 2 + m_rows * c_len * 4
                                + 8 * m_rows * kc * 4),
        name="attention_ctx",
    )(qkv, qkv, qkv)
    return o_lat, o_ctx


def _moe_kernel(layer, tm, f, te_ref, tv_ref, nx_ref, nv_ref, src_ref, h_hbm, w1_hbm, w3_hbm, w2_hbm,
                y_ref, xbuf, st1, st3, st2, w13b, w2b, sem_x, sem_w):
    i = pl.program_id(0)
    slot = i % MOE_SLOTS
    valid = tv_ref[i] == 1

    def row_copy(t, r, sl):
        row = src_ref[t * tm + r]
        return pltpu.make_async_copy(h_hbm.at[pl.ds(row, 1)], xbuf.at[sl, pl.ds(r, 1)], sem_x.at[sl])

    def wait_rows(sl):
        pltpu.make_async_copy(h_hbm.at[pl.ds(0, tm)], xbuf.at[sl], sem_x.at[sl]).wait()

    def weight_copies(e):
        return (pltpu.make_async_copy(w1_hbm.at[layer, e], st1, sem_w.at[0]),
                pltpu.make_async_copy(w3_hbm.at[layer, e], st3, sem_w.at[1]),
                pltpu.make_async_copy(w2_hbm.at[layer, e], st2, sem_w.at[2]))

    @pl.when(jnp.logical_and(i == 0, valid))
    def _():
        for t in range(MOE_LOOKAHEAD):
            def body(r, _, t=t):
                row_copy(t, r, t).start()
                return 0
            lax.fori_loop(0, tm, body, 0, unroll=8)
        for cp in weight_copies(te_ref[0]):
            cp.start(priority=1)

    new_expert = jnp.logical_or(i == 0, te_ref[i] != te_ref[jnp.maximum(i - 1, 0)])

    @pl.when(jnp.logical_and(new_expert, valid))
    def _():
        for cp in weight_copies(te_ref[i]):
            cp.wait()
        w13b[:, :f] = st1[...].astype(BF16)
        w13b[:, f:] = st3[...].astype(BF16)
        w2b[...] = st2[...].astype(BF16)

        @pl.when(nx_ref[i] >= 0)
        def _():
            for cp in weight_copies(nx_ref[i]):
                cp.start(priority=1)

    @pl.when(valid)
    def _():
        wait_rows(slot)
        x = _unpack_bf16_pairs(xbuf[slot])
        h13 = jnp.dot(x, w13b[...], preferred_element_type=F32)
        h1 = h13[:, :f]
        act = ((h1 * _sigmoid(h1)) * h13[:, f:]).astype(BF16)
        nslot = (i + MOE_LOOKAHEAD) % MOE_SLOTS
        for r in range(tm):
            row_copy(i + MOE_LOOKAHEAD, r, nslot).start()
        y_ref[...] = _pack_bf16_pairs(jnp.dot(act, w2b[...], preferred_element_type=F32))

    @pl.when(jnp.logical_not(valid))
    def _():
        y_ref[...] = jnp.zeros_like(y_ref)

        @pl.when(i < nv_ref[0] + MOE_LOOKAHEAD)
        def _():
            wait_rows(slot)


def moe_experts(hp, tile_expert, tile_valid, next_expert, n_valid, src_rows, w1, w3, w2, layer, *, tm):
    half = hp.shape[1]
    d = 2 * half
    p = src_rows.shape[0]
    f = w2.shape[2]
    nt = p // tm
    any_spec = pl.BlockSpec(memory_space=pl.ANY)
    grid_spec = pltpu.PrefetchScalarGridSpec(
        num_scalar_prefetch=5,
        grid=(nt,),
        in_specs=[any_spec, any_spec, any_spec, any_spec],
        out_specs=pl.BlockSpec((tm, half), lambda i, te, tv, nx, nv, src: (i, 0)),
        scratch_shapes=[pltpu.VMEM((MOE_SLOTS, tm, half), jnp.uint32),
                        pltpu.VMEM((d, f), F32), pltpu.VMEM((d, f), F32), pltpu.VMEM((f, d), F32),
                        pltpu.VMEM((d, 2 * f), BF16), pltpu.VMEM((f, d), BF16),
                        pltpu.SemaphoreType.DMA((MOE_SLOTS,)), pltpu.SemaphoreType.DMA((3,))],
    )
    return pl.pallas_call(
        functools.partial(_moe_kernel, layer, tm, f),
        grid_spec=grid_spec,
        out_shape=jax.ShapeDtypeStruct((p, half), jnp.uint32),
        compiler_params=_params(("arbitrary",),
                                MOE_SLOTS * tm * half * 4 + 3 * d * f * 4 + 3 * d * f * 2 + 2 * tm * d * 4
                                + tm * d * 6 + tm * 2 * f * 8),
        name="moe_experts",
    )(tile_expert, tile_valid, next_expert, n_valid, src_rows, hp, w1, w3, w2)


def _combine_kernel(tm, mode, pos_ref, x_ref, g_ref, r_ref, y_hbm, *rest):
    if mode == "mid":
        gs_ref, sh_ref, o_ref, h_ref, ybuf, sem = rest
    else:
        gs_ref, o_ref, ybuf, sem = rest
    i = pl.program_id(0)
    nt = pl.num_programs(0)
    slot = i % 2

    def issue(t, sl):
        def body(r, _):
            for k in range(2):
                row = pos_ref[2 * (t * tm + r) + k]
                pltpu.make_async_copy(y_hbm.at[pl.ds(row, 1)], ybuf.at[sl, k, pl.ds(r, 1)],
                                      sem.at[sl]).start()
            return 0
        lax.fori_loop(0, tm, body, 0, unroll=4)

    @pl.when(i == 0)
    def _():
        issue(0, 0)

    @pl.when(i + 1 < nt)
    def _():
        issue(i + 1, 1 - slot)

    for k in range(2):
        pltpu.make_async_copy(y_hbm.at[pl.ds(0, tm)], ybuf.at[slot, k], sem.at[slot]).wait()
    rinfo = r_ref[...]
    moe = (rinfo[:, 2:3] * _unpack_bf16_pairs(ybuf[slot, 0], F32)
           + rinfo[:, 3:4] * _unpack_bf16_pairs(ybuf[slot, 1], F32))
    x_new = x_ref[...] + g_ref[...] * moe
    if mode == "mid":
        o_ref[...] = x_new
        h_ref[...] = _modulate(x_new, gs_ref[...], sh_ref[...]).astype(h_ref.dtype)
    else:
        ms = jnp.mean(x_new * x_new, axis=-1, keepdims=True)
        o_ref[...] = x_new * lax.rsqrt(ms + EPS) * gs_ref[...]


def moe_combine(x, gate, rinfo, y, pos, gs, sh, *, n_rows, s_len, n_batch, tm=128):
    d = x.shape[1]
    mode = "mid" if sh is not None else "last"
    assert n_rows % tm == 0 and n_rows <= x.shape[0]
    cls = _cls_of_tile(tm, s_len, s_len * n_batch, n_batch)
    row_spec = pl.BlockSpec((tm, d), lambda i, pos: (i, 0))
    cls_spec = pl.BlockSpec((None, 1, d), lambda i, pos: (cls(i), 0, 0))
    in_specs = [row_spec, cls_spec, pl.BlockSpec((tm, LANES), lambda i, pos: (i, 0)),
                pl.BlockSpec(memory_space=pl.ANY)]
    if mode == "mid":
        in_specs += [cls_spec, cls_spec]
        operands = (gs, sh)
        out_specs = [row_spec, row_spec]
        out_shape = [jax.ShapeDtypeStruct((n_rows, d), F32), jax.ShapeDtypeStruct((n_rows, d), BF16)]
    else:
        in_specs += [pl.BlockSpec((1, d), lambda i, pos: (0, 0))]
        operands = (gs.reshape(1, d),)
        out_specs = row_spec
        out_shape = jax.ShapeDtypeStruct((n_rows, d), F32)
    grid_spec = pltpu.PrefetchScalarGridSpec(
        num_scalar_prefetch=1,
        grid=(n_rows // tm,),
        in_specs=in_specs,
        out_specs=out_specs,
        scratch_shapes=[pltpu.VMEM((2, 2, tm, d // 2), jnp.uint32), pltpu.SemaphoreType.DMA((2,))],
    )
    return pl.pallas_call(
        functools.partial(_combine_kernel, tm, mode),
        grid_spec=grid_spec,
        out_shape=out_shape,
        compiler_params=_params(("arbitrary",), 4 * tm * d * 4 + 6 * tm * d * 4 + 4 * tm * d * 4),
        name="moe_combine_" + mode,
    )(pos, x, gate, rinfo, y, *operands)


def _route_plan(rinfo, n_experts, tm):
    r = rinfo.shape[0]
    e = rinfo[:, 0:2].astype(jnp.int32).reshape(-1)
    n_assign = 2 * r
    onehot = (e[:, None] == jnp.arange(n_experts, dtype=jnp.int32)[None, :]).astype(jnp.int32)
    csum = jnp.cumsum(onehot, axis=0)
    rank = jnp.sum(csum * onehot, axis=1) - 1
    counts = csum[-1]
    ntile = (counts + tm - 1) // tm
    tend = jnp.cumsum(ntile)
    tstart = tend - ntile
    pos = tstart[e] * tm + rank
    n_tiles = n_assign // tm + n_experts + MOE_LOOKAHEAD
    p = n_tiles * tm
    src = jnp.zeros((p,), jnp.int32).at[pos].set(jnp.arange(n_assign, dtype=jnp.int32) // 2)
    tid = jnp.arange(n_tiles, dtype=jnp.int32)
    tile_valid = (tid < tend[-1]).astype(jnp.int32)
    tile_expert = jnp.minimum(jnp.sum((tid[:, None] >= tend[None, :]).astype(jnp.int32), axis=1),
                              n_experts - 1)
    nxt = tend[tile_expert]
    next_expert = jnp.where(nxt < tend[-1], tile_expert[jnp.minimum(nxt, n_tiles - 1)], -1)
    last = jnp.maximum(tend[-1] - 1, 0)
    tile_expert = jnp.where(tile_valid == 1, tile_expert, tile_expert[last])
    return pos, src, tile_expert, tile_valid, next_expert.astype(jnp.int32), tend[-1:].astype(jnp.int32)


def _rope_tables(s_len, hd, tm):
    rows = s_len // ROPE_GRID_W
    t_row = jnp.repeat(jnp.arange(rows), ROPE_GRID_W).astype(F32)
    t_col = jnp.tile(jnp.arange(ROPE_GRID_W), rows).astype(F32)
    n_f = hd // 4
    inv = ROPE_THETA ** (-jnp.arange(n_f, dtype=F32) / n_f)
    ang = jnp.concatenate([t_row[:, None] * inv, t_col[:, None] * inv], axis=-1)
    cos, sin = jnp.cos(ang), jnp.sin(ang)
    cos2 = jnp.concatenate([cos, cos], axis=-1)
    sin2 = jnp.concatenate([-sin, sin], axis=-1)
    cos2 = jnp.concatenate([cos2, jnp.ones((tm, hd), F32)], axis=0)
    sin2 = jnp.concatenate([sin2, jnp.zeros((tm, hd), F32)], axis=0)
    return cos2, sin2


def kernel(x, c, ctx, c_ctx, ada_w, ada_b, norm_mix, norm_ffn, final_norm_g, lru_w_in, lru_conv_w,
           lru_conv_b, lru_w_a, lru_b_a, lru_w_i, lru_b_i, lru_lam, lru_w_out, attn_w_qkv, attn_q_norm,
           attn_k_norm, attn_w_o, moe_w_rg, moe_b_rg, moe_w_re, moe_b_re, moe_w1, moe_w3, moe_w2):
    n_batch, s_len, d = x.shape
    c_len = ctx.shape[1]
    depth = ada_w.shape[0]
    n_ada = ada_w.shape[2] // d
    d_rnn = lru_w_in.shape[2] // 2
    hd = attn_q_norm.shape[1]
    n_heads = d // hd
    n_kv = (attn_w_qkv.shape[2] // hd - n_heads) // 2
    n_groups = moe_w_rg.shape[2]
    n_experts = moe_w_re.shape[2]
    per_group = n_experts // n_groups
    n_lat = n_batch * s_len
    tm = 512 if (s_len % 512 == 0 and (n_batch * c_len) % 512 == 0) else 256
    moe_tm = 256
    geo = dict(s_len=s_len, n_batch=n_batch)
    assert n_batch + 1 <= NCLS_PAD and n_groups + n_experts <= LANES

    cond = jnp.zeros((NCLS_PAD, d), F32).at[:n_batch].set(c).at[n_batch].set(c_ctx)
    mods = ada_all(cond, ada_w, ada_b).reshape(depth, NCLS_PAD, n_ada, d)
    cos_tbl, sin_tbl = _rope_tables(s_len, hd, tm)

    def cls_vec(v):
        return v.reshape(NCLS_PAD, 1, d)

    n_mixers = 2
    n_all = n_lat + n_batch * c_len

    def mix_mod(l):
        return cls_vec(norm_mix[l] * (1.0 + mods[l, :, 1])), cls_vec(mods[l, :, 0])

    xs, h = modulate_first(x.reshape(n_lat, d), ctx.reshape(n_batch * c_len, d), *mix_mod(0), **geo)
    for l in range(depth):
        last = l == depth - 1
        sh_m, sc_m, g_m, sh_f, sc_f, g_f = [mods[l, :, j] for j in range(n_ada)]
        j = l // n_mixers
        if l % n_mixers == 0:
            yg = matmul(h, lru_w_in, j, n_cols=d_rnn, col0=0, out_dtype=BF16, act="gelu", tm=tm)
            xr = matmul(h, lru_w_in, j, n_cols=d_rnn, col0=d_rnn, out_dtype=F32, tm=tm)
            z_lat, z_ctx = lru_mix(xr, yg, j, lru_conv_w, lru_conv_b, lru_w_a, lru_w_i, lru_b_a,
                                   lru_b_i, lru_lam, c_len=c_len, **geo)
            xs = matmul_resid(z_lat, lru_w_out, j, xs, cls_vec(g_m),
                              a_tail=None if last else z_ctx, tm=tm, **geo)
        else:
            q_scale = hd ** -0.5 * math.log2(math.e)
            gain = jnp.concatenate([jnp.tile(attn_q_norm[j] * q_scale, n_heads),
                                    jnp.tile(attn_k_norm[j], n_kv),
                                    jnp.ones((n_kv * hd,), F32)]).reshape(1, -1)
            qkv = matmul_qkv(h, attn_w_qkv, j, gain, cos_tbl, sin_tbl,
                             n_norm_cols=(n_heads + n_kv) * hd, hd=hd, tm=tm, **geo)
            o_lat, o_ctx = attention(qkv, n_heads=n_heads, n_kv=n_kv, hd=hd, c_len=c_len, d_out=d,
                                     with_ctx=not last, **geo)
            xs = matmul_resid(o_lat, attn_w_o, j, xs, cls_vec(g_m), a_tail=o_ctx, tm=tm, **geo)

        n_rows = n_lat if last else n_all
        wr = jnp.zeros((d, LANES), F32).at[:, :n_groups].set(moe_w_rg[l])
        wr = wr.at[:, n_groups:n_groups + n_experts].set(moe_w_re[l])
        br = jnp.zeros((1, LANES), F32).at[0, :n_groups].set(moe_b_rg[l])
        br = br.at[0, n_groups:n_groups + n_experts].set(moe_b_re[l])
        wr_hi = wr.astype(BF16)
        wr = jnp.concatenate([wr_hi, (wr - wr_hi.astype(F32)).astype(BF16)], axis=1)
        hp, rinfo = modulate_router(xs, cls_vec(norm_ffn[l] * (1.0 + sc_f)), cls_vec(sh_f), wr, br,
                                    n_rows=n_rows, n_groups=n_groups, per_group=per_group, **geo)
        pos, src, tile_expert, tile_valid, next_expert, n_valid = _route_plan(rinfo, n_experts, moe_tm)
        y = moe_experts(hp, tile_expert, tile_valid, next_expert, n_valid, src, moe_w1, moe_w3, moe_w2,
                        l, tm=moe_tm)
        if last:
            out = moe_combine(xs, cls_vec(g_f), rinfo, y, pos, final_norm_g, None, n_rows=n_rows, **geo)
        else:
            xs, h = moe_combine(xs, cls_vec(g_f), rinfo, y, pos, *mix_mod(l + 1), n_rows=n_rows, **geo)

    return out.reshape(n_batch, s_len, d)
```

```python
import functools
import math

import jax
import jax.numpy as jnp
from jax import lax
from jax.experimental import pallas as pl
from jax.experimental.pallas import tpu as pltpu

EPS = 1e-6
LRU_C = 8.0
ROPE_THETA = 10000.0
ROPE_GRID_W = 64
LANES = 128
SUBLANES = 8
NCLS_PAD = 8
V7X_VMEM_CAP = 56 * 1024 * 1024
MOE_LOOKAHEAD = 2
MOE_SLOTS = MOE_LOOKAHEAD + 1
BF16 = jnp.bfloat16
F32 = jnp.float32


def _vmem_limit(nbytes):
    return int(min(V7X_VMEM_CAP, max(16 * 1024 * 1024, nbytes * 5 // 4 + (4 << 20))))


def _params(sem, nbytes):
    return pltpu.CompilerParams(dimension_semantics=sem, vmem_limit_bytes=_vmem_limit(nbytes))


def _sigmoid(x):
    return 1.0 / (1.0 + jnp.exp(-x))


def _gelu_tanh(x):
    return 0.5 * x * (1.0 + jnp.tanh(math.sqrt(2.0 / math.pi) * (x + 0.044715 * (x * x * x))))


def _ada_kernel(c_ref, w_ref, b_ref, o_ref):
    c = c_ref[...]
    s = (c * _sigmoid(c)).astype(BF16)
    w = w_ref[...].astype(BF16)
    o_ref[...] = jnp.dot(s, w, preferred_element_type=F32) + b_ref[...]


def ada_all(cond, ada_w, ada_b):
    depth, d, n = ada_w.shape
    tn = 1024 if n % 1024 == 0 else 512
    return pl.pallas_call(
        _ada_kernel,
        grid=(depth, n // tn),
        in_specs=[
            pl.BlockSpec((NCLS_PAD, d), lambda l, j: (0, 0)),
            pl.BlockSpec((None, d, tn), lambda l, j: (l, 0, j)),
            pl.BlockSpec((None, 1, tn), lambda l, j: (l, 0, j)),
        ],
        out_specs=pl.BlockSpec((None, NCLS_PAD, tn), lambda l, j: (l, 0, j)),
        out_shape=jax.ShapeDtypeStruct((depth, NCLS_PAD, n), F32),
        compiler_params=_params(("arbitrary", "arbitrary"), 2 * d * tn * 4 + d * tn * 2),
        name="ada_all",
    )(cond, ada_w, ada_b.reshape(depth, 1, n))


def _modulate(x, gs, sh):
    ms = jnp.mean(x * x, axis=-1, keepdims=True)
    return x * lax.rsqrt(ms + EPS) * gs + sh


def _pack_bf16_pairs(h):
    half = h.shape[1] // 2
    bits = lax.bitcast_convert_type(h.astype(BF16).astype(F32), jnp.uint32)
    return (bits[:, :half] >> 16) | (bits[:, half:] & jnp.uint32(0xFFFF0000))


def _unpack_bf16_pairs(words, dtype=BF16):
    lo = lax.bitcast_convert_type(words << 16, F32).astype(dtype)
    hi = lax.bitcast_convert_type(words & jnp.uint32(0xFFFF0000), F32).astype(dtype)
    return jnp.concatenate([lo, hi], axis=1)


def _mod_router_kernel(n_groups, per_group, x_ref, gs_ref, sh_ref, wr_ref, br_ref, h_ref, r_ref):
    h = _modulate(x_ref[...], gs_ref[...], sh_ref[...])
    h_ref[...] = _pack_bf16_pairs(h)
    hh = h.astype(BF16)
    hl = (h - hh.astype(F32)).astype(BF16)
    r_hi = jnp.dot(hh, wr_ref[...], preferred_element_type=F32)
    r_lo = jnp.dot(hl, wr_ref[:, :LANES], preferred_element_type=F32)
    logits = r_hi[:, :LANES] + r_hi[:, LANES:] + r_lo + br_ref[...]
    lane = lax.broadcasted_iota(jnp.int32, logits.shape, 1)
    neg = jnp.float32(-jnp.inf)
    big = jnp.int32(1 << 20)
    gl = jnp.where(lane < n_groups, logits, neg)
    gmax = jnp.max(gl, axis=-1, keepdims=True)
    gsum = jnp.sum(jnp.exp(gl - gmax), axis=-1, keepdims=True)
    g_top = 1.0 / gsum
    g_idx = jnp.min(jnp.where(gl == gmax, lane, big), axis=-1, keepdims=True)
    lo = n_groups + per_group * g_idx
    el = jnp.where((lane >= lo) & (lane < lo + per_group), logits, neg)
    m1 = jnp.max(el, axis=-1, keepdims=True)
    i1 = jnp.min(jnp.where(el == m1, lane, big), axis=-1, keepdims=True)
    el2 = jnp.where(lane == i1, neg, el)
    m2 = jnp.max(el2, axis=-1, keepdims=True)
    i2 = jnp.min(jnp.where(el2 == m2, lane, big), axis=-1, keepdims=True)
    t = jnp.exp(m2 - m1)
    w1 = g_top / (1.0 + t)
    w2 = g_top * t / (1.0 + t)
    e1 = (i1 - n_groups).astype(F32)
    e2 = (i2 - n_groups).astype(F32)
    r_ref[...] = jnp.where(lane == 0, e1, jnp.where(lane == 1, e2,
                           jnp.where(lane == 2, w1, jnp.where(lane == 3, w2, 0.0))))


def _mod_first_kernel(n_lat_tiles, xl_ref, xc_ref, gs_ref, sh_ref, xs_ref, h_ref):
    def emit(src_ref):
        x = src_ref[...]
        xs_ref[...] = x
        h_ref[...] = _modulate(x, gs_ref[...], sh_ref[...]).astype(h_ref.dtype)

    @pl.when(pl.program_id(0) < n_lat_tiles)
    def _():
        emit(xl_ref)

    @pl.when(pl.program_id(0) >= n_lat_tiles)
    def _():
        emit(xc_ref)


def _cls_of_tile(tm, s_len, n_lat, n_batch):
    def f(i):
        r0 = i * tm
        return jnp.where(r0 < n_lat, r0 // s_len, n_batch)
    return f


def modulate_first(x_lat, x_ctx, gs, sh, *, s_len, n_batch, tm=256):
    n_lat, d = x_lat.shape
    n_ctx = x_ctx.shape[0]
    assert n_lat % tm == 0 and n_ctx % tm == 0
    lt, ct = n_lat // tm, n_ctx // tm
    cls = _cls_of_tile(tm, s_len, n_lat, n_batch)
    row_spec = pl.BlockSpec((tm, d), lambda i: (i, 0))
    return pl.pallas_call(
        functools.partial(_mod_first_kernel, lt),
        grid=(lt + ct,),
        in_specs=[
            pl.BlockSpec((tm, d), lambda i: (jnp.minimum(i, lt - 1), 0)),
            pl.BlockSpec((tm, d), lambda i: (jnp.clip(i - lt, 0, ct - 1), 0)),
            pl.BlockSpec((None, 1, d), lambda i: (cls(i), 0, 0)),
            pl.BlockSpec((None, 1, d), lambda i: (cls(i), 0, 0)),
        ],
        out_specs=[row_spec, row_spec],
        out_shape=[jax.ShapeDtypeStruct((n_lat + n_ctx, d), F32),
                   jax.ShapeDtypeStruct((n_lat + n_ctx, d), BF16)],
        compiler_params=_params(("arbitrary",), 2 * tm * d * (4 + 4 + 4 + 2)),
        name="modulate_first",
    )(x_lat, x_ctx, gs, sh)


def modulate_router(x, gs, sh, wr, br, *, n_rows, n_groups, per_group, s_len, n_batch, tm=256):
    r, d = n_rows, x.shape[1]
    assert r % tm == 0 and r <= x.shape[0]
    cls = _cls_of_tile(tm, s_len, s_len * n_batch, n_batch)
    return pl.pallas_call(
        functools.partial(_mod_router_kernel, n_groups, per_group),
        grid=(r // tm,),
        in_specs=[
            pl.BlockSpec((tm, d), lambda i: (i, 0)),
            pl.BlockSpec((None, 1, d), lambda i: (cls(i), 0, 0)),
            pl.BlockSpec((None, 1, d), lambda i: (cls(i), 0, 0)),
            pl.BlockSpec((d, 2 * LANES), lambda i: (0, 0)),
            pl.BlockSpec((1, LANES), lambda i: (0, 0)),
        ],
        out_specs=[
            pl.BlockSpec((tm, d // 2), lambda i: (i, 0)),
            pl.BlockSpec((tm, LANES), lambda i: (i, 0)),
        ],
        out_shape=[jax.ShapeDtypeStruct((r, d // 2), jnp.uint32),
                   jax.ShapeDtypeStruct((r, LANES), F32)],
        compiler_params=_params(("arbitrary",), 2 * tm * d * 8 + 2 * d * LANES * 4 + 4 * tm * d * 4),
        name="modulate_router",
    )(x, gs, sh, wr, br)


def _cast_weight_once(wsel, w_hbm, wbf_ref, stage_ref, sem):
    layer, col0, tn = wsel
    j = pl.program_id(0)

    def copy(jj):
        start = col0 + jj * tn
        if not isinstance(start, int):
            start = pl.multiple_of(start, tn)
        return pltpu.make_async_copy(w_hbm.at[layer, :, pl.ds(start, tn)], stage_ref, sem.at[0])

    @pl.when(pl.program_id(1) == 0)
    def _():
        @pl.when(j == 0)
        def _():
            copy(0).start()

        copy(j).wait()
        wbf_ref[...] = stage_ref[...].astype(BF16)

        @pl.when(j + 1 < pl.num_programs(0))
        def _():
            copy(j + 1).start()


def _mm_plain_kernel(wsel, act, a_ref, w_hbm, o_ref, wbf_ref, stage_ref, sem):
    _cast_weight_once(wsel, w_hbm, wbf_ref, stage_ref, sem)
    acc = jnp.dot(a_ref[...], wbf_ref[...], preferred_element_type=F32)
    if act == "gelu":
        acc = _gelu_tanh(acc)
    o_ref[...] = acc.astype(o_ref.dtype)


def _mm_resid_kernel(wsel, n_main_tiles, a_ref, a2_ref, w_hbm, x_ref, g_ref, o_ref, wbf_ref, stage_ref,
                     sem):
    _cast_weight_once(wsel, w_hbm, wbf_ref, stage_ref, sem)
    i = pl.program_id(1)

    def emit(src_ref):
        acc = jnp.dot(src_ref[...], wbf_ref[...], preferred_element_type=F32)
        o_ref[...] = x_ref[...] + g_ref[...] * acc

    @pl.when(i < n_main_tiles)
    def _():
        emit(a_ref)

    @pl.when(i >= n_main_tiles)
    def _():
        emit(a2_ref)


def _mm_qkv_kernel(wsel, n_norm_tiles, hd, a_ref, w_hbm, gain_ref, cos_ref, sin_ref, o_ref, wbf_ref,
                   stage_ref, sem):
    _cast_weight_once(wsel, w_hbm, wbf_ref, stage_ref, sem)
    acc = jnp.dot(a_ref[...], wbf_ref[...], preferred_element_type=F32)
    j = pl.program_id(0)

    @pl.when(j < n_norm_tiles)
    def _():
        cos = cos_ref[...]
        sin = sin_ref[...]
        for h in range(acc.shape[1] // hd):
            xh = acc[:, h * hd:(h + 1) * hd]
            ms = jnp.mean(xh * xh, axis=-1, keepdims=True)
            y = xh * lax.rsqrt(ms + EPS) * gain_ref[:, h * hd:(h + 1) * hd]
            y = y * cos + pltpu.roll(y, hd // 2, 1) * sin
            o_ref[:, h * hd:(h + 1) * hd] = y.astype(o_ref.dtype)

    @pl.when(j >= n_norm_tiles)
    def _():
        o_ref[...] = acc.astype(o_ref.dtype)


_W_HBM = pl.BlockSpec(memory_space=pl.ANY)


def _w_scratch(k, tn):
    return [pltpu.VMEM((k, tn), BF16), pltpu.VMEM((k, tn), F32), pltpu.SemaphoreType.DMA((1,))]


def _mm_vmem(tm, k, tn, out_bytes, extra=0):
    return (2 * (tm * k * 2 + tm * tn * out_bytes) + k * tn * 4 + k * tn * 2 + 2 * tm * tn * 4
            + extra)


def matmul(a, w, layer, *, n_cols, col0=0, out_dtype, act=None, tm=512, tn=1024):
    r, k = a.shape
    tn = min(tn, n_cols)
    assert r % tm == 0 and n_cols % tn == 0 and col0 % tn == 0
    ob = jnp.dtype(out_dtype).itemsize
    return pl.pallas_call(
        functools.partial(_mm_plain_kernel, (layer, col0, tn), act),
        grid=(n_cols // tn, r // tm),
        in_specs=[pl.BlockSpec((tm, k), lambda j, i: (i, 0)), _W_HBM],
        out_specs=pl.BlockSpec((tm, tn), lambda j, i: (i, j)),
        out_shape=jax.ShapeDtypeStruct((r, n_cols), out_dtype),
        scratch_shapes=_w_scratch(k, tn),
        compiler_params=_params(("arbitrary", "arbitrary"), _mm_vmem(tm, k, tn, ob)),
        name="matmul_" + (act or "plain"),
    )(a, w)


def matmul_resid(a, w, layer, xres, gate, *, s_len, n_batch, a_tail=None, tm=512, tn=1024):
    n = xres.shape[1]
    k = a.shape[1]
    tn = min(tn, n)
    n_main = a.shape[0] // tm
    if a_tail is None:
        a_tail, n_tail, r = a, 1, a.shape[0]
    else:
        n_tail, r = a_tail.shape[0] // tm, a.shape[0] + a_tail.shape[0]
    assert r % tm == 0 and n % tn == 0 and a.shape[0] % tm == 0 and r <= xres.shape[0]
    cls = _cls_of_tile(tm, s_len, s_len * n_batch, n_batch)
    return pl.pallas_call(
        functools.partial(_mm_resid_kernel, (layer, 0, tn), n_main),
        grid=(n // tn, r // tm),
        in_specs=[pl.BlockSpec((tm, k), lambda j, i: (jnp.minimum(i, n_main - 1), 0)),
                  pl.BlockSpec((tm, k), lambda j, i: (jnp.clip(i - n_main, 0, n_tail - 1), 0),
                               pipeline_mode=pl.Buffered(1)),
                  _W_HBM,
                  pl.BlockSpec((tm, tn), lambda j, i: (i, j)),
                  pl.BlockSpec((None, 1, tn), lambda j, i: (cls(i), 0, j))],
        out_specs=pl.BlockSpec((tm, tn), lambda j, i: (i, j)),
        out_shape=jax.ShapeDtypeStruct((r, n), F32),
        scratch_shapes=_w_scratch(k, tn),
        compiler_params=_params(("arbitrary", "arbitrary"),
                                _mm_vmem(tm, k, tn, 4, tm * k * 2 + 2 * tm * tn * 4)),
        name="matmul_resid",
    )(a, a_tail, w, xres, gate)


def matmul_qkv(a, w, layer, gain, cos_tbl, sin_tbl, *, n_norm_cols, hd, s_len, n_batch, tm=512, tn=1024):
    r, k = a.shape
    n = w.shape[2]
    while n_norm_cols % tn or n % tn:
        tn //= 2
    assert tn % hd == 0 and s_len % tm == 0
    n_lat = s_len * n_batch
    pos_blocks = s_len // tm

    def pos(i):
        return jnp.where(i * tm < n_lat, i % pos_blocks, pos_blocks)

    return pl.pallas_call(
        functools.partial(_mm_qkv_kernel, (layer, 0, tn), n_norm_cols // tn, hd),
        grid=(n // tn, r // tm),
        in_specs=[pl.BlockSpec((tm, k), lambda j, i: (i, 0)),
                  _W_HBM,
                  pl.BlockSpec((1, tn), lambda j, i: (0, j)),
                  pl.BlockSpec((tm, hd), lambda j, i: (pos(i), 0)),
                  pl.BlockSpec((tm, hd), lambda j, i: (pos(i), 0))],
        out_specs=pl.BlockSpec((tm, tn), lambda j, i: (i, j)),
        out_shape=jax.ShapeDtypeStruct((r, n), BF16),
        scratch_shapes=_w_scratch(k, tn),
        compiler_params=_params(("arbitrary", "arbitrary"),
                                _mm_vmem(tm, k, tn, 2, 4 * tm * hd * 4 + tm * tn * 4)),
        name="matmul_qkv",
    )(a, w, gain, cos_tbl, sin_tbl)


def _seg_geometry(t_len):
    ls = t_len // SUBLANES
    assert ls % SUBLANES == 0
    return ls, ls + SUBLANES


def _lru_kernel(tc, xl_ref, xc_ref, yl_ref, yc_ref, cw_ref, cb_ref, wa_ref, wi_ref, ba_ref, bi_ref,
                lam_ref, ol_ref, oc_ref, cvl_ref, cvc_ref, accl_ref, accc_ref,
                al_ref, ul_ref, ac_ref, uc_ref):
    w = xl_ref.shape[1]
    cw = cw_ref[...]
    cb = cb_ref[...]

    def conv_seq(x_ref, cv_ref):
        t_len = x_ref.shape[0]
        nchunk = t_len // tc

        def body(ci, _):
            t0 = pl.multiple_of(ci * tc, tc)
            main = x_ref[pl.ds(t0, tc), :]
            p0 = pl.multiple_of(jnp.maximum(t0 - 8, 0), 8)
            n0 = pl.multiple_of(jnp.minimum(t0 + tc, t_len - 8), 8)
            prev8 = jnp.where(ci > 0, x_ref[pl.ds(p0, 8), :], 0.0)
            next8 = jnp.where(ci < nchunk - 1, x_ref[pl.ds(n0, 8), :], 0.0)
            ext = jnp.concatenate([prev8, main, next8], axis=0)
            ne = tc + 16
            xm1 = pltpu.roll(ext, 1, 0)[8:8 + tc]
            xp1 = pltpu.roll(ext, ne - 1, 0)[8:8 + tc]
            xp2 = pltpu.roll(ext, ne - 2, 0)[8:8 + tc]
            cv_ref[pl.ds(t0, tc), :] = (cw[0:1] * xm1 + cw[1:2] * main + cw[2:3] * xp1
                                        + cw[3:4] * xp2 + cb)
            return 0

        lax.fori_loop(0, nchunk, body, 0)

    conv_seq(xl_ref, cvl_ref)
    conv_seq(xc_ref, cvc_ref)

    def run_seq(d, cv_ref, acc_ref, a_ref, u_ref, carry_in):
        reverse = d == 1
        t_len = cv_ref.shape[0]
        ls, stride = _seg_geometry(t_len)
        piece = min(tc, ls)
        n_lt = w // LANES
        wa, wi, ba, bi, lam = wa_ref[d], wi_ref[d], ba_ref[d], bi_ref[d], lam_ref[d]
        nsp = -LRU_C * (jnp.maximum(-lam, 0.0) + jnp.log(1.0 + jnp.exp(-jnp.abs(lam))))
        c0 = (0.5 * math.log2(math.e)) * nsp

        def seg_row(t):
            return pl.multiple_of(t + SUBLANES * (t // ls), SUBLANES)

        def coeff_body(ci, _):
            t0 = pl.multiple_of(ci * tc, tc)
            x = cv_ref[pl.ds(t0, tc), :]
            xb = x.astype(BF16)
            tr = jnp.tanh(jnp.dot(xb, wa, preferred_element_type=F32) + ba)
            ti = jnp.tanh(jnp.dot(xb, wi, preferred_element_type=F32) + bi)
            a = jnp.exp2(c0 + c0 * tr)
            u = jnp.sqrt(1.0 - a * a) * ((0.5 + 0.5 * ti) * x)
            for p in range(tc // piece):
                row = seg_row(t0 + p * piece)
                for lt in range(n_lt):
                    lanes = slice(lt * LANES, (lt + 1) * LANES)
                    a_ref[lt, pl.ds(row, piece), :] = a[p * piece:(p + 1) * piece, lanes]
                    u_ref[lt, pl.ds(row, piece), :] = u[p * piece:(p + 1) * piece, lanes]
            return 0

        lax.fori_loop(0, t_len // tc, coeff_body, 0, unroll=4 if (t_len // tc) % 4 == 0 else 1)

        def scan_body(k, carry):
            r = (ls - 1 - k) if reverse else k
            idx = pl.ds(r, SUBLANES, stride=stride)
            out = []
            for lt in range(n_lt):
                h, p = carry[lt]
                a = a_ref[lt, idx, :]
                h = a * h + u_ref[lt, idx, :]
                p = a * p
                u_ref[lt, idx, :] = h
                a_ref[lt, idx, :] = p
                out.append((h, p))
            return tuple(out)

        init = tuple((jnp.zeros((SUBLANES, LANES), F32), jnp.ones((SUBLANES, LANES), F32))
                     for _ in range(n_lt))
        ends = lax.fori_loop(0, ls, scan_body, init, unroll=8)
        h_end = jnp.concatenate([e[0] for e in ends], axis=1)
        p_end = jnp.concatenate([e[1] for e in ends], axis=1)

        c = carry_in
        seg_in = [None] * SUBLANES
        for s in (range(SUBLANES - 1, -1, -1) if reverse else range(SUBLANES)):
            seg_in[s] = c
            c = p_end[s:s + 1] * c + h_end[s:s + 1]
        for s in range(SUBLANES):
            for q in range(ls // piece):
                src = s * stride + q * piece
                dst = s * ls + q * piece
                for lt in range(n_lt):
                    lanes = slice(lt * LANES, (lt + 1) * LANES)
                    h = (u_ref[lt, src:src + piece, :]
                         + a_ref[lt, src:src + piece, :] * seg_in[s][:, lanes])
                    if d == 0:
                        acc_ref[dst:dst + piece, lanes] = h
                    else:
                        acc_ref[dst:dst + piece, lanes] = acc_ref[dst:dst + piece, lanes] + h
        return c

    for d in range(2):
        carry = run_seq(d, cvc_ref, accc_ref, ac_ref, uc_ref, jnp.zeros((1, w), F32))
        run_seq(d, cvl_ref, accl_ref, al_ref, ul_ref, carry)

    def finish(acc_ref, y_ref, o_ref):
        nchunk = acc_ref.shape[0] // tc

        def body(ci, _):
            t0 = pl.multiple_of(ci * tc, tc)
            o_ref[pl.ds(t0, tc), :] = (acc_ref[pl.ds(t0, tc), :]
                                       * y_ref[pl.ds(t0, tc), :].astype(F32)).astype(o_ref.dtype)
            return 0

        lax.fori_loop(0, nchunk, body, 0)

    finish(accl_ref, yl_ref, ol_ref)
    finish(accc_ref, yc_ref, oc_ref)


def lru_mix(xr, yg, layer, conv_w, conv_b, w_a, w_i, b_a, b_i, lam, *, s_len, c_len, n_batch, tc=256):
    r, rn = xr.shape
    nblk, bw = w_a.shape[2], w_a.shape[3]
    assert s_len % tc == 0 and c_len % tc == 0 and (n_batch * s_len) % c_len == 0
    ctx0 = n_batch * s_len // c_len
    lat = lambda b, k: (b, k)
    ctx = lambda b, k: (ctx0 + b, k)
    vec = lambda b, k: (layer, 0, 0, k)
    seg_rows = lambda t: SUBLANES * _seg_geometry(t)[1]
    seg_buf = lambda t: pltpu.VMEM((bw // LANES, seg_rows(t), LANES), F32)
    n_dir = w_a.shape[1]
    return pl.pallas_call(
        functools.partial(_lru_kernel, tc),
        grid=(n_batch, nblk),
        in_specs=[
            pl.BlockSpec((s_len, bw), lat), pl.BlockSpec((c_len, bw), ctx),
            pl.BlockSpec((s_len, bw), lat), pl.BlockSpec((c_len, bw), ctx),
            pl.BlockSpec((None, conv_w.shape[1], bw), lambda b, k: (layer, 0, k)),
            pl.BlockSpec((None, 1, bw), lambda b, k: (layer, 0, k)),
            pl.BlockSpec((None, n_dir, None, bw, bw), lambda b, k: (layer, 0, k, 0, 0)),
            pl.BlockSpec((None, n_dir, None, bw, bw), lambda b, k: (layer, 0, k, 0, 0)),
            pl.BlockSpec((None, n_dir, 1, bw), vec), pl.BlockSpec((None, n_dir, 1, bw), vec),
            pl.BlockSpec((None, n_dir, 1, bw), vec),
        ],
        out_specs=[pl.BlockSpec((s_len, bw), lambda b, k: (b, k)),
                   pl.BlockSpec((c_len, bw), lambda b, k: (b, k))],
        out_shape=[jax.ShapeDtypeStruct((n_batch * s_len, rn), BF16),
                   jax.ShapeDtypeStruct((n_batch * c_len, rn), BF16)],
        scratch_shapes=[pltpu.VMEM((s_len, bw), F32), pltpu.VMEM((c_len, bw), F32),
                        pltpu.VMEM((s_len, bw), F32), pltpu.VMEM((c_len, bw), F32),
                        seg_buf(s_len), seg_buf(s_len), seg_buf(c_len), seg_buf(c_len)],
        compiler_params=_params(("arbitrary", "arbitrary"),
                                (s_len + c_len) * bw * (2 * 4 + 2 * 2 + 2 * 2 + 8)
                                + 2 * (seg_rows(s_len) + seg_rows(c_len)) * bw * 4
                                + 8 * bw * bw + 24 * tc * bw * 4),
        name="lru_mix",
    )(xr, xr, yg, yg, conv_w, conv_b.reshape(conv_b.shape[0], 1, rn),
      (0.5 * w_a).astype(BF16), (0.5 * w_i).astype(BF16),
      (0.5 * b_a).reshape(-1, n_dir, 1, rn), (0.5 * b_i).reshape(-1, n_dir, 1, rn),
      lam.reshape(-1, n_dir, 1, rn))


_DN_NT = (((1,), (1,)), ((), ()))


def _split_heads(q, groups, hd):
    return jnp.concatenate([q[:, g * hd:(g + 1) * hd] for g in range(groups)], axis=0)


def _attn_pass1(q4, k_refs, s_ref, m_ref, kc):
    m_rows = q4.shape[0]
    mx = jnp.full((m_rows, LANES), -jnp.inf, F32)
    off = 0
    for k_ref in k_refs:
        for c0 in range(0, k_ref.shape[0], kc):
            w = min(kc, k_ref.shape[0] - c0)
            s = lax.dot_general(q4, k_ref[c0:c0 + w, :], _DN_NT, preferred_element_type=F32)
            s_ref[:, off:off + w] = s
            for t in range(w // LANES):
                mx = jnp.maximum(mx, s[:, t * LANES:(t + 1) * LANES])
            off += w
    m_ref[...] = jnp.broadcast_to(jnp.max(mx, axis=-1, keepdims=True), (m_rows, LANES))


def _attn_pass2(v_refs, s_ref, m_ref, kc):
    m_rows = s_ref.shape[0]
    hd = v_refs[0].shape[1]
    mrow = m_ref[...]
    lsum = jnp.zeros((m_rows, LANES), F32)
    o = jnp.zeros((m_rows, hd), F32)
    off = 0
    for v_ref in v_refs:
        for c0 in range(0, v_ref.shape[0], kc):
            w = min(kc, v_ref.shape[0] - c0)
            ps = []
            for t in range(w // LANES):
                p = jnp.exp2(s_ref[:, off + t * LANES:off + (t + 1) * LANES] - mrow)
                lsum = lsum + p
                ps.append(p.astype(BF16))
            p_all = jnp.concatenate(ps, axis=1) if len(ps) > 1 else ps[0]
            o = o + jnp.dot(p_all, v_ref[c0:c0 + w, :], preferred_element_type=F32)
            off += w
    return o / jnp.sum(lsum, axis=-1, keepdims=True)


def _emit_heads(o, o_ref, row0, tq, groups, hd):
    for g in range(groups):
        o_ref[row0:row0 + tq, g * hd:(g + 1) * hd] = o[g * tq:(g + 1) * tq].astype(o_ref.dtype)


def _attn_lat_kernel(tq, nb, groups, hd, kc, qc_ref, qn_ref, kl_ref, vl_ref, kcx_ref, vcx_ref, o_ref,
                     s0_ref, s1_ref, m0_ref, m1_ref):
    keys = [kcx_ref, kl_ref]
    vals = [vcx_ref, vl_ref]
    bufs = [(s0_ref, m0_ref), (s1_ref, m1_ref)]

    @pl.when(pl.program_id(2) == 0)
    def _():
        _attn_pass1(_split_heads(qc_ref[0:tq, :], groups, hd), keys, s0_ref, m0_ref, kc)

    for b in range(nb):
        q_next = qc_ref[(b + 1) * tq:(b + 2) * tq, :] if b + 1 < nb else qn_ref[0:tq, :]
        _attn_pass1(_split_heads(q_next, groups, hd), keys, *bufs[(b + 1) % 2], kc)
        _emit_heads(_attn_pass2(vals, *bufs[b % 2], kc), o_ref, b * tq, tq, groups, hd)


def _attn_ctx_kernel(tq, groups, hd, kc, q_ref, k_ref, v_ref, o_ref, s_ref, m_ref):
    _attn_pass1(_split_heads(q_ref[...], groups, hd), [k_ref], s_ref, m_ref, kc)
    _emit_heads(_attn_pass2([v_ref], s_ref, m_ref, kc), o_ref, 0, tq, groups, hd)


def attention(qkv, *, n_heads, n_kv, hd, s_len, c_len, n_batch, d_out, with_ctx=True, tq=128, kc=512):
    groups = n_heads // n_kv
    gw = groups * hd
    m_rows = groups * tq
    nb = 4 if s_len % (4 * tq) == 0 else 2
    assert s_len % (nb * tq) == 0 and c_len % tq == 0 and (n_batch * s_len) % c_len == 0
    n_pair = s_len // (nb * tq)
    ctx_k0 = n_batch * s_len // c_len
    kv_bytes = 4 * (s_len + c_len) * hd * 2
    o_lat = pl.pallas_call(
        functools.partial(_attn_lat_kernel, tq, nb, groups, hd, kc),
        grid=(n_batch, n_kv, n_pair),
        in_specs=[
            pl.BlockSpec((nb * tq, gw), lambda b, kv, i: (b * n_pair + i, kv)),
            pl.BlockSpec((nb * tq, gw), lambda b, kv, i: (b * n_pair + jnp.minimum(i + 1, n_pair - 1), kv)),
            pl.BlockSpec((s_len, hd), lambda b, kv, i: (b, n_heads + kv)),
            pl.BlockSpec((s_len, hd), lambda b, kv, i: (b, n_heads + n_kv + kv)),
            pl.BlockSpec((c_len, hd), lambda b, kv, i: (ctx_k0 + b, n_heads + kv)),
            pl.BlockSpec((c_len, hd), lambda b, kv, i: (ctx_k0 + b, n_heads + n_kv + kv)),
        ],
        out_specs=pl.BlockSpec((nb * tq, gw), lambda b, kv, i: (b * n_pair + i, kv)),
        out_shape=jax.ShapeDtypeStruct((n_batch * s_len, d_out), BF16),
        scratch_shapes=[pltpu.VMEM((m_rows, s_len + c_len), F32), pltpu.VMEM((m_rows, s_len + c_len), F32),
                        pltpu.VMEM((m_rows, LANES), F32), pltpu.VMEM((m_rows, LANES), F32)],
        compiler_params=_params(("arbitrary", "arbitrary", "arbitrary"),
                                kv_bytes + 12 * tq * gw * 2 + 2 * m_rows * (s_len + c_len) * 4
                                + 8 * m_rows * kc * 4),
        name="attention_lat",
    )(qkv, qkv, qkv, qkv, qkv, qkv)

    if not with_ctx:
        return o_lat, None
    nq_ctx = c_len // tq
    ctx_q0 = n_batch * s_len // tq
    o_ctx = pl.pallas_call(
        functools.partial(_attn_ctx_kernel, tq, groups, hd, kc),
        grid=(n_batch, n_kv, nq_ctx),
        in_specs=[
            pl.BlockSpec((tq, gw), lambda b, kv, i: (ctx_q0 + b * nq_ctx + i, kv)),
            pl.BlockSpec((c_len, hd), lambda b, kv, i: (ctx_k0 + b, n_heads + kv)),
            pl.BlockSpec((c_len, hd), lambda b, kv, i: (ctx_k0 + b, n_heads + n_kv + kv)),
        ],
        out_specs=pl.BlockSpec((tq, gw), lambda b, kv, i: (b * nq_ctx + i, kv)),
        out_shape=jax.ShapeDtypeStruct((n_batch * c_len, d_out), BF16),
        scratch_shapes=[pltpu.VMEM((m_rows, c_len), F32), pltpu.VMEM((m_rows, LANES), F32)],
        compiler_params=_params(("arbitrary", "arbitrary", "arbitrary"),
                                4 * c_len * hd * 2 + 4 * tq * gw * 2 + m_rows * c_len * 4
                                + 8 * m_rows * kc * 4),
        name="attention_ctx",
    )(qkv, qkv, qkv)
    return o_lat, o_ctx


def _moe_kernel(layer, tm, f, te_ref, tv_ref, nx_ref, nv_ref, src_ref, h_hbm, w1_hbm, w3_hbm, w2_hbm,
                y_ref, xbuf, st1, st3, st2, w13b, w2b, sem_x, sem_w):
    i = pl.program_id(0)
    slot = i % MOE_SLOTS
    valid = tv_ref[i] == 1

    def row_copy(t, r, sl):
        row = src_ref[t * tm + r]
        return pltpu.make_async_copy(h_hbm.at[pl.ds(row, 1)], xbuf.at[sl, pl.ds(r, 1)], sem_x.at[sl])

    def wait_rows(sl):
        pltpu.make_async_copy(h_hbm.at[pl.ds(0, tm)], xbuf.at[sl], sem_x.at[sl]).wait()

    def weight_copies(e):
        return (pltpu.make_async_copy(w1_hbm.at[layer, e], st1, sem_w.at[0]),
                pltpu.make_async_copy(w3_hbm.at[layer, e], st3, sem_w.at[1]),
                pltpu.make_async_copy(w2_hbm.at[layer, e], st2, sem_w.at[2]))

    @pl.when(jnp.logical_and(i == 0, valid))
    def _():
        for t in range(MOE_LOOKAHEAD):
            def body(r, _, t=t):
                row_copy(t, r, t).start()
                return 0
            lax.fori_loop(0, tm, body, 0, unroll=8)
        for cp in weight_copies(te_ref[0]):
            cp.start(priority=1)

    new_expert = jnp.logical_or(i == 0, te_ref[i] != te_ref[jnp.maximum(i - 1, 0)])

    @pl.when(jnp.logical_and(new_expert, valid))
    def _():
        for cp in weight_copies(te_ref[i]):
            cp.wait()
        w13b[:, :f] = st1[...].astype(BF16)
        w13b[:, f:] = st3[...].astype(BF16)
        w2b[...] = st2[...].astype(BF16)

        @pl.when(nx_ref[i] >= 0)
        def _():
            for cp in weight_copies(nx_ref[i]):
                cp.start(priority=1)

    @pl.when(valid)
    def _():
        wait_rows(slot)
        x = _unpack_bf16_pairs(xbuf[slot])
        h13 = jnp.dot(x, w13b[...], preferred_element_type=F32)
        h1 = h13[:, :f]
        act = ((h1 * _sigmoid(h1)) * h13[:, f:]).astype(BF16)
        nslot = (i + MOE_LOOKAHEAD) % MOE_SLOTS
        for r in range(tm):
            row_copy(i + MOE_LOOKAHEAD, r, nslot).start()
        y_ref[...] = _pack_bf16_pairs(jnp.dot(act, w2b[...], preferred_element_type=F32))

    @pl.when(jnp.logical_not(valid))
    def _():
        y_ref[...] = jnp.zeros_like(y_ref)

        @pl.when(i < nv_ref[0] + MOE_LOOKAHEAD)
        def _():
            wait_rows(slot)


def moe_experts(hp, tile_expert, tile_valid, next_expert, n_valid, src_rows, w1, w3, w2, layer, *, tm):
    half = hp.shape[1]
    d = 2 * half
    p = src_rows.shape[0]
    f = w2.shape[2]
    nt = p // tm
    any_spec = pl.BlockSpec(memory_space=pl.ANY)
    grid_spec = pltpu.PrefetchScalarGridSpec(
        num_scalar_prefetch=5,
        grid=(nt,),
        in_specs=[any_spec, any_spec, any_spec, any_spec],
        out_specs=pl.BlockSpec((tm, half), lambda i, te, tv, nx, nv, src: (i, 0)),
        scratch_shapes=[pltpu.VMEM((MOE_SLOTS, tm, half), jnp.uint32),
                        pltpu.VMEM((d, f), F32), pltpu.VMEM((d, f), F32), pltpu.VMEM((f, d), F32),
                        pltpu.VMEM((d, 2 * f), BF16), pltpu.VMEM((f, d), BF16),
                        pltpu.SemaphoreType.DMA((MOE_SLOTS,)), pltpu.SemaphoreType.DMA((3,))],
    )
    return pl.pallas_call(
        functools.partial(_moe_kernel, layer, tm, f),
        grid_spec=grid_spec,
        out_shape=jax.ShapeDtypeStruct((p, half), jnp.uint32),
        compiler_params=_params(("arbitrary",),
                                MOE_SLOTS * tm * half * 4 + 3 * d * f * 4 + 3 * d * f * 2 + 2 * tm * d * 4
                                + tm * d * 6 + tm * 2 * f * 8),
        name="moe_experts",
    )(tile_expert, tile_valid, next_expert, n_valid, src_rows, hp, w1, w3, w2)


def _combine_kernel(tm, mode, pos_ref, x_ref, g_ref, r_ref, y_hbm, *rest):
    if mode == "mid":
        gs_ref, sh_ref, o_ref, h_ref, ybuf, sem = rest
    else:
        gs_ref, o_ref, ybuf, sem = rest
    i = pl.program_id(0)
    nt = pl.num_programs(0)
    slot = i % 2

    def issue(t, sl):
        def body(r, _):
            for k in range(2):
                row = pos_ref[2 * (t * tm + r) + k]
                pltpu.make_async_copy(y_hbm.at[pl.ds(row, 1)], ybuf.at[sl, k, pl.ds(r, 1)],
                                      sem.at[sl]).start()
            return 0
        lax.fori_loop(0, tm, body, 0, unroll=4)

    @pl.when(i == 0)
    def _():
        issue(0, 0)

    @pl.when(i + 1 < nt)
    def _():
        issue(i + 1, 1 - slot)

    for k in range(2):
        pltpu.make_async_copy(y_hbm.at[pl.ds(0, tm)], ybuf.at[slot, k], sem.at[slot]).wait()
    rinfo = r_ref[...]
    moe = (rinfo[:, 2:3] * _unpack_bf16_pairs(ybuf[slot, 0], F32)
           + rinfo[:, 3:4] * _unpack_bf16_pairs(ybuf[slot, 1], F32))
    x_new = x_ref[...] + g_ref[...] * moe
    if mode == "mid":
        o_ref[...] = x_new
        h_ref[...] = _modulate(x_new, gs_ref[...], sh_ref[...]).astype(h_ref.dtype)
    else:
        ms = jnp.mean(x_new * x_new, axis=-1, keepdims=True)
        o_ref[...] = x_new * lax.rsqrt(ms + EPS) * gs_ref[...]


def moe_combine(x, gate, rinfo, y, pos, gs, sh, *, n_rows, s_len, n_batch, tm=128):
    d = x.shape[1]
    mode = "mid" if sh is not None else "last"
    assert n_rows % tm == 0 and n_rows <= x.shape[0]
    cls = _cls_of_tile(tm, s_len, s_len * n_batch, n_batch)
    row_spec = pl.BlockSpec((tm, d), lambda i, pos: (i, 0))
    cls_spec = pl.BlockSpec((None, 1, d), lambda i, pos: (cls(i), 0, 0))
    in_specs = [row_spec, cls_spec, pl.BlockSpec((tm, LANES), lambda i, pos: (i, 0)),
                pl.BlockSpec(memory_space=pl.ANY)]
    if mode == "mid":
        in_specs += [cls_spec, cls_spec]
        operands = (gs, sh)
        out_specs = [row_spec, row_spec]
        out_shape = [jax.ShapeDtypeStruct((n_rows, d), F32), jax.ShapeDtypeStruct((n_rows, d), BF16)]
    else:
        in_specs += [pl.BlockSpec((1, d), lambda i, pos: (0, 0))]
        operands = (gs.reshape(1, d),)
        out_specs = row_spec
        out_shape = jax.ShapeDtypeStruct((n_rows, d), F32)
    grid_spec = pltpu.PrefetchScalarGridSpec(
        num_scalar_prefetch=1,
        grid=(n_rows // tm,),
        in_specs=in_specs,
        out_specs=out_specs,
        scratch_shapes=[pltpu.VMEM((2, 2, tm, d // 2), jnp.uint32), pltpu.SemaphoreType.DMA((2,))],
    )
    return pl.pallas_call(
        functools.partial(_combine_kernel, tm, mode),
        grid_spec=grid_spec,
        out_shape=out_shape,
        compiler_params=_params(("arbitrary",), 4 * tm * d * 4 + 6 * tm * d * 4 + 4 * tm * d * 4),
        name="moe_combine_" + mode,
    )(pos, x, gate, rinfo, y, *operands)


def _route_plan(rinfo, n_experts, tm):
    r = rinfo.shape[0]
    e = rinfo[:, 0:2].astype(jnp.int32).reshape(-1)
    n_assign = 2 * r
    onehot = (e[:, None] == jnp.arange(n_experts, dtype=jnp.int32)[None, :]).astype(jnp.int32)
    csum = jnp.cumsum(onehot, axis=0)
    rank = jnp.sum(csum * onehot, axis=1) - 1
    counts = csum[-1]
    ntile = (counts + tm - 1) // tm
    tend = jnp.cumsum(ntile)
    tstart = tend - ntile
    pos = tstart[e] * tm + rank
    n_tiles = n_assign // tm + n_experts + MOE_LOOKAHEAD
    p = n_tiles * tm
    src = jnp.zeros((p,), jnp.int32).at[pos].set(jnp.arange(n_assign, dtype=jnp.int32) // 2)
    tid = jnp.arange(n_tiles, dtype=jnp.int32)
    tile_valid = (tid < tend[-1]).astype(jnp.int32)
    tile_expert = jnp.minimum(jnp.sum((tid[:, None] >= tend[None, :]).astype(jnp.int32), axis=1),
                              n_experts - 1)
    nxt = tend[tile_expert]
    next_expert = jnp.where(nxt < tend[-1], tile_expert[jnp.minimum(nxt, n_tiles - 1)], -1)
    last = jnp.maximum(tend[-1] - 1, 0)
    tile_expert = jnp.where(tile_valid == 1, tile_expert, tile_expert[last])
    return pos, src, tile_expert, tile_valid, next_expert.astype(jnp.int32), tend[-1:].astype(jnp.int32)


def _rope_tables(s_len, hd, tm):
    rows = s_len // ROPE_GRID_W
    t_row = jnp.repeat(jnp.arange(rows), ROPE_GRID_W).astype(F32)
    t_col = jnp.tile(jnp.arange(ROPE_GRID_W), rows).astype(F32)
    n_f = hd // 4
    inv = ROPE_THETA ** (-jnp.arange(n_f, dtype=F32) / n_f)
    ang = jnp.concatenate([t_row[:, None] * inv, t_col[:, None] * inv], axis=-1)
    cos, sin = jnp.cos(ang), jnp.sin(ang)
    cos2 = jnp.concatenate([cos, cos], axis=-1)
    sin2 = jnp.concatenate([-sin, sin], axis=-1)
    cos2 = jnp.concatenate([cos2, jnp.ones((tm, hd), F32)], axis=0)
    sin2 = jnp.concatenate([sin2, jnp.zeros((tm, hd), F32)], axis=0)
    return cos2, sin2


def kernel(x, c, ctx, c_ctx, ada_w, ada_b, norm_mix, norm_ffn, final_norm_g, lru_w_in, lru_conv_w,
           lru_conv_b, lru_w_a, lru_b_a, lru_w_i, lru_b_i, lru_lam, lru_w_out, attn_w_qkv, attn_q_norm,
           attn_k_norm, attn_w_o, moe_w_rg, moe_b_rg, moe_w_re, moe_b_re, moe_w1, moe_w3, moe_w2):
    n_batch, s_len, d = x.shape
    c_len = ctx.shape[1]
    depth = ada_w.shape[0]
    n_ada = ada_w.shape[2] // d
    d_rnn = lru_w_in.shape[2] // 2
    hd = attn_q_norm.shape[1]
    n_heads = d // hd
    n_kv = (attn_w_qkv.shape[2] // hd - n_heads) // 2
    n_groups = moe_w_rg.shape[2]
    n_experts = moe_w_re.shape[2]
    per_group = n_experts // n_groups
    n_lat = n_batch * s_len
    tm = 512 if (s_len % 512 == 0 and (n_batch * c_len) % 512 == 0) else 256
    moe_tm = 256
    geo = dict(s_len=s_len, n_batch=n_batch)
    assert n_batch + 1 <= NCLS_PAD and n_groups + n_experts <= LANES

    cond = jnp.zeros((NCLS_PAD, d), F32).at[:n_batch].set(c).at[n_batch].set(c_ctx)
    mods = ada_all(cond, ada_w, ada_b).reshape(depth, NCLS_PAD, n_ada, d)
    cos_tbl, sin_tbl = _rope_tables(s_len, hd, tm)

    def cls_vec(v):
        return v.reshape(NCLS_PAD, 1, d)

    n_mixers = 2
    n_all = n_lat + n_batch * c_len

    def mix_mod(l):
        return cls_vec(norm_mix[l] * (1.0 + mods[l, :, 1])), cls_vec(mods[l, :, 0])

    xs, h = modulate_first(x.reshape(n_lat, d), ctx.reshape(n_batch * c_len, d), *mix_mod(0), **geo)
    for l in range(depth):
        last = l == depth - 1
        sh_m, sc_m, g_m, sh_f, sc_f, g_f = [mods[l, :, j] for j in range(n_ada)]
        j = l // n_mixers
        if l % n_mixers == 0:
            yg = matmul(h, lru_w_in, j, n_cols=d_rnn, col0=0, out_dtype=BF16, act="gelu", tm=tm)
            xr = matmul(h, lru_w_in, j, n_cols=d_rnn, col0=d_rnn, out_dtype=F32, tm=tm)
            z_lat, z_ctx = lru_mix(xr, yg, j, lru_conv_w, lru_conv_b, lru_w_a, lru_w_i, lru_b_a,
                                   lru_b_i, lru_lam, c_len=c_len, **geo)
            xs = matmul_resid(z_lat, lru_w_out, j, xs, cls_vec(g_m),
                              a_tail=None if last else z_ctx, tm=tm, **geo)
        else:
            q_scale = hd ** -0.5 * math.log2(math.e)
            gain = jnp.concatenate([jnp.tile(attn_q_norm[j] * q_scale, n_heads),
                                    jnp.tile(attn_k_norm[j], n_kv),
                                    jnp.ones((n_kv * hd,), F32)]).reshape(1, -1)
            qkv = matmul_qkv(h, attn_w_qkv, j, gain, cos_tbl, sin_tbl,
                             n_norm_cols=(n_heads + n_kv) * hd, hd=hd, tm=tm, **geo)
            o_lat, o_ctx = attention(qkv, n_heads=n_heads, n_kv=n_kv, hd=hd, c_len=c_len, d_out=d,
                                     with_ctx=not last, **geo)
            xs = matmul_resid(o_lat, attn_w_o, j, xs, cls_vec(g_m), a_tail=o_ctx, tm=tm, **geo)

        n_rows = n_lat if last else n_all
        wr = jnp.zeros((d, LANES), F32).at[:, :n_groups].set(moe_w_rg[l])
        wr = wr.at[:, n_groups:n_groups + n_experts].set(moe_w_re[l])
        br = jnp.zeros((1, LANES), F32).at[0, :n_groups].set(moe_b_rg[l])
        br = br.at[0, n_groups:n_groups + n_experts].set(moe_b_re[l])
        wr_hi = wr.astype(BF16)
        wr = jnp.concatenate([wr_hi, (wr - wr_hi.astype(F32)).astype(BF16)], axis=1)
        hp, rinfo = modulate_router(xs, cls_vec(norm_ffn[l] * (1.0 + sc_f)), cls_vec(sh_f), wr, br,
                                    n_rows=n_rows, n_groups=n_groups, per_group=per_group, **geo)
        pos, src, tile_expert, tile_valid, next_expert, n_valid = _route_plan(rinfo, n_experts, moe_tm)
        y = moe_experts(hp, tile_expert, tile_valid, next_expert, n_valid, src, moe_w1, moe_w3, moe_w2,
                        l, tm=moe_tm)
        if last:
            out = moe_combine(xs, cls_vec(g_f), rinfo, y, pos, final_norm_g, None, n_rows=n_rows, **geo)
        else:
            xs, h = moe_combine(xs, cls_vec(g_f), rinfo, y, pos, *mix_mod(l + 1), n_rows=n_rows, **geo)

    return out.reshape(n_batch, s_len, d)
```
